```python
import jax, jax.numpy as jnp
from jax import lax
import numpy as np

D_MODEL = 2048
BATCH = 4
SEQ = 4096
DEPTH = 2

CHUNK = 64
Q_BLOCK = 128
MEM_LEN = 256
N_A_LAYERS = DEPTH // 2
N_B_LAYERS = DEPTH - N_A_LAYERS

MIX_WIDTH = D_MODEL
MEM_HEADS = 4
MEM_HEAD_DIM = D_MODEL // 16
MEM_WIDTH = MEM_HEADS * MEM_HEAD_DIM
TOK_WIDTH = MIX_WIDTH - MEM_WIDTH

LRU_WIDTH = TOK_WIDTH
LRU_BLOCKS = 12
LRU_BLOCK_DIM = LRU_WIDTH // LRU_BLOCKS
CONV_WIDTH = 4
LRU_C = 8.0

V_HEAD_DIM = 128
MLA_HEADS = TOK_WIDTH // V_HEAD_DIM
QK_NOPE_DIM = 128
QK_ROPE_DIM = 64
QK_HEAD_DIM = QK_NOPE_DIM + QK_ROPE_DIM
Q_LORA_RANK = 768
KV_LORA_RANK = 512
ROPE_THETA = 10000.0

A_IN_WIDTH = 2 * LRU_WIDTH + MEM_WIDTH
B_IN_WIDTH = Q_LORA_RANK + MEM_WIDTH

D_FF = 4 * D_MODEL
EPS = 1e-6

kernel_name = "yoco_rglru_mla_memory_trunk"


def rms_norm(x, g):
    xf = x.astype(jnp.float32)
    xf = xf * lax.rsqrt(jnp.mean(xf * xf, axis=-1, keepdims=True) + EPS)
    return xf.astype(x.dtype) * g


def rope_tables(positions):
    half = QK_ROPE_DIM // 2
    inv_freq = ROPE_THETA ** (-jnp.arange(half, dtype=jnp.float32) / half)
    ang = positions.astype(jnp.float32)[..., None] * inv_freq
    return jnp.cos(ang), jnp.sin(ang)


def apply_rope(x, cos, sin):
    half = QK_ROPE_DIM // 2
    x1, x2 = x[..., :half], x[..., half:]
    c = cos.astype(x.dtype)
    s = sin.astype(x.dtype)
    return jnp.concatenate([x1 * c - x2 * s, x2 * c + x1 * s], axis=-1)


def rglru_mixer(xb, gate, conv_w, conv_b, w_r, b_r, w_i, b_i, lam):
    B, S, C = xb.shape
    xc = lax.conv_general_dilated(
        xb, conv_w[:, None, :], window_strides=(1,), padding=[(CONV_WIDTH - 1, 0)],
        dimension_numbers=('NWC', 'WIO', 'NWC'), feature_group_count=C) + conv_b
    xh = xc.reshape(B, S, LRU_BLOCKS, LRU_BLOCK_DIM)
    r = jax.nn.sigmoid(jnp.einsum('bsnd,nde->bsne', xh, w_r).reshape(B, S, C) + b_r)
    i = jax.nn.sigmoid(jnp.einsum('bsnd,nde->bsne', xh, w_i).reshape(B, S, C) + b_i)
    log_a = (-LRU_C * r.astype(jnp.float32)) * jax.nn.softplus(-lam.astype(jnp.float32))
    a = jnp.exp(log_a)
    b = jnp.sqrt(-jnp.expm1(2.0 * log_a)) * (i * xc).astype(jnp.float32)

    def combine(left, right):
        return (left[0] * right[0], right[0] * left[1] + right[1])

    _, h = lax.associative_scan(combine, (a, b), axis=1)
    return h.astype(xb.dtype) * jax.nn.gelu(gate)


def shared_mla_kv(h, g_in, w_down, g_latent, w_up, cos, sin):
    B, S, _ = h.shape
    ckv = rms_norm(h, g_in) @ w_down
    latent = rms_norm(ckv[..., :KV_LORA_RANK], g_latent)
    k_pe = apply_rope(ckv[..., KV_LORA_RANK:], cos, sin)
    kv = (latent @ w_up).reshape(B, S, MLA_HEADS, QK_NOPE_DIM + V_HEAD_DIM)
    return kv[..., :QK_NOPE_DIM], k_pe, kv[..., QK_NOPE_DIM:]


def mla_block_causal(q_nope, q_pe, k_nope, k_pe, v):
    B, S, H, _ = q_nope.shape
    scale = QK_HEAD_DIM ** -0.5
    outs = []
    for blk in range(S // Q_BLOCK):
        q0 = blk * Q_BLOCK
        kl = q0 + Q_BLOCK
        s = (jnp.einsum('bqhd,bkhd->bhqk', q_nope[:, q0:kl], k_nope[:, :kl])
             + jnp.einsum('bqhd,bkd->bhqk', q_pe[:, q0:kl], k_pe[:, :kl])).astype(jnp.float32) * scale
        q_chunk = (q0 + jnp.arange(Q_BLOCK)) // CHUNK
        k_chunk = jnp.arange(kl) // CHUNK
        mask = k_chunk[None, :] <= q_chunk[:, None]
        p = jax.nn.softmax(jnp.where(mask, s, -1e30), axis=-1).astype(v.dtype)
        outs.append(jnp.einsum('bhqk,bkhd->bqhd', p, v[:, :kl]))
    return jnp.concatenate(outs, axis=1).reshape(B, S, H * V_HEAD_DIM)


def memory_attention(qm, mk, mv):
    B, S, _ = qm.shape
    q = qm.reshape(B, S, MEM_HEADS, MEM_HEAD_DIM)
    s = jnp.einsum('bshd,bmhd->bhsm', q, mk).astype(jnp.float32) * (MEM_HEAD_DIM ** -0.5)
    p = jax.nn.softmax(s, axis=-1).astype(mv.dtype)
    return jnp.einsum('bhsm,bmhd->bshd', p, mv).reshape(B, S, MEM_WIDTH)


def setup_inputs(seed: int = 0) -> dict:
    key = jax.random.key(seed)
    ks = iter(jax.random.split(key, 40))
    nrm = lambda shape, scale: jax.random.normal(next(ks), shape, jnp.float32) * scale
    gain = lambda shape: 1.0 + 0.05 * jax.random.normal(next(ks), shape, jnp.float32)

    x = jax.random.normal(next(ks), (BATCH, SEQ, D_MODEL), jnp.float32)
    mem = jax.random.normal(next(ks), (BATCH, MEM_LEN, D_MODEL), jnp.float32)
    offset = jax.random.randint(next(ks), (BATCH, 1), 0, 64) * CHUNK
    positions = (offset + jnp.arange(SEQ)[None, :]).astype(jnp.int32)

    u = jax.random.uniform(next(ks), (N_A_LAYERS, LRU_WIDTH), jnp.float32, minval=0.9, maxval=0.999)
    a0 = u ** (1.0 / LRU_C)
    a_lambda = jnp.log(a0) - jnp.log1p(-a0)

    return {
        "x": x, "mem": mem, "positions": positions,
        "g_mix_pre": gain((DEPTH, D_MODEL)),
        "g_mix_post": gain((DEPTH, D_MODEL)),
        "g_mlp_pre": gain((DEPTH, D_MODEL)),
        "g_mlp_post": gain((DEPTH, D_MODEL)),
        "g_mem": gain((DEPTH, D_MODEL)),
        "w_mem_k": nrm((DEPTH, D_MODEL, MEM_WIDTH), D_MODEL ** -0.5),
        "w_mem_v": nrm((DEPTH, D_MODEL, MEM_WIDTH), D_MODEL ** -0.5),
        "w_o": nrm((DEPTH, MIX_WIDTH, D_MODEL), MIX_WIDTH ** -0.5),
        "w_ff1": nrm((DEPTH, D_MODEL, D_FF), D_MODEL ** -0.5),
        "w_ff2": nrm((DEPTH, D_FF, D_MODEL), D_FF ** -0.5),
        "a_w_in": nrm((N_A_LAYERS, D_MODEL, A_IN_WIDTH), D_MODEL ** -0.5),
        "a_conv_w": nrm((N_A_LAYERS, CONV_WIDTH, LRU_WIDTH), CONV_WIDTH ** -0.5),
        "a_conv_b": nrm((N_A_LAYERS, LRU_WIDTH), 0.01),
        "a_w_rgate": nrm((N_A_LAYERS, LRU_BLOCKS, LRU_BLOCK_DIM, LRU_BLOCK_DIM), LRU_BLOCK_DIM ** -0.5),
        "a_b_rgate": nrm((N_A_LAYERS, LRU_WIDTH), 0.01),
        "a_w_igate": nrm((N_A_LAYERS, LRU_BLOCKS, LRU_BLOCK_DIM, LRU_BLOCK_DIM), LRU_BLOCK_DIM ** -0.5),
        "a_b_igate": nrm((N_A_LAYERS, LRU_WIDTH), 0.01),
        "a_lambda": a_lambda,
        "b_w_in": nrm((N_B_LAYERS, D_MODEL, B_IN_WIDTH), D_MODEL ** -0.5),
        "b_g_qa": gain((N_B_LAYERS, Q_LORA_RANK)),
        "b_w_qb": nrm((N_B_LAYERS, Q_LORA_RANK, MLA_HEADS * QK_HEAD_DIM), Q_LORA_RANK ** -0.5),
        "kv_g_in": gain((D_MODEL,)),
        "kv_w_down": nrm((D_MODEL, KV_LORA_RANK + QK_ROPE_DIM), D_MODEL ** -0.5),
        "kv_g_latent": gain((KV_LORA_RANK,)),
        "kv_w_up": nrm((KV_LORA_RANK, MLA_HEADS * (QK_NOPE_DIM + V_HEAD_DIM)), KV_LORA_RANK ** -0.5),
    }


def reference(x, mem, positions,
              g_mix_pre, g_mix_post, g_mlp_pre, g_mlp_post, g_mem, w_mem_k, w_mem_v, w_o, w_ff1, w_ff2,
              a_w_in, a_conv_w, a_conv_b, a_w_rgate, a_b_rgate, a_w_igate, a_b_igate, a_lambda,
              b_w_in, b_g_qa, b_w_qb,
              kv_g_in, kv_w_down, kv_g_latent, kv_w_up):
    B, S, _ = x.shape
    cos, sin = rope_tables(positions)
    h = x
    shared = None
    for layer in range(DEPTH):
        hn = rms_norm(h, g_mix_pre[layer])
        mn = rms_norm(mem, g_mem[layer])
        mk = (mn @ w_mem_k[layer]).reshape(B, MEM_LEN, MEM_HEADS, MEM_HEAD_DIM)
        mv = (mn @ w_mem_v[layer]).reshape(B, MEM_LEN, MEM_HEADS, MEM_HEAD_DIM)
        if layer < N_A_LAYERS:
            la = layer
            proj = hn @ a_w_in[la]
            xb = proj[..., :LRU_WIDTH]
            gate = proj[..., LRU_WIDTH:2 * LRU_WIDTH]
            qm = proj[..., 2 * LRU_WIDTH:]
            tok = rglru_mixer(xb, gate, a_conv_w[la], a_conv_b[la], a_w_rgate[la], a_b_rgate[la],
                              a_w_igate[la], a_b_igate[la], a_lambda[la])
        else:
            lb = layer - N_A_LAYERS
            if shared is None:
                shared = shared_mla_kv(h, kv_g_in, kv_w_down, kv_g_latent, kv_w_up, cos, sin)
            k_nope, k_pe, v = shared
            proj = hn @ b_w_in[lb]
            cq = rms_norm(proj[..., :Q_LORA_RANK], b_g_qa[lb])
            qm = proj[..., Q_LORA_RANK:]
            q = (cq @ b_w_qb[lb]).reshape(B, S, MLA_HEADS, QK_HEAD_DIM)
            q_nope = q[..., :QK_NOPE_DIM]
            q_pe = apply_rope(q[..., QK_NOPE_DIM:], cos[:, :, None, :], sin[:, :, None, :])
            tok = mla_block_causal(q_nope, q_pe, k_nope, k_pe, v)
        mem_out = memory_attention(qm, mk, mv)
        y = jnp.concatenate([tok, mem_out], axis=-1) @ w_o[layer]
        h = h + rms_norm(y, g_mix_post[layer])
        f = rms_norm(h, g_mlp_pre[layer]) @ w_ff1[layer]
        f = jnp.square(jax.nn.relu(f)) @ w_ff2[layer]
        h = h + rms_norm(f, g_mlp_post[layer])
    return h
```

```python
import functools

import jax
import jax.numpy as jnp
from jax import lax
from jax.experimental import pallas as pl
from jax.experimental.pallas import tpu as pltpu

EPS = 1e-6
LRU_C = 8.0
CHUNK = 64
ROPE_THETA = 10000.0
MEM_HEADS = 4
HEAD_DIM = 128
ROPE_DIM = 64
QK_HEAD_DIM = HEAD_DIM + ROPE_DIM
QK_PAD = 256
SUBLANES = 8
VMEM_LIMIT = 56 * 1024 * 1024

BF16 = jnp.bfloat16
F32 = jnp.float32


def _params(sem, vmem=VMEM_LIMIT):
    return pltpu.CompilerParams(dimension_semantics=sem, vmem_limit_bytes=vmem)


def _rms(x, g):
    return x * lax.rsqrt(jnp.mean(x * x, axis=-1, keepdims=True) + EPS) * g


def _dot(a, b):
    return jnp.dot(a, b, preferred_element_type=F32)


def _dot_nt(a, b):
    return lax.dot_general(a, b, (((1,), (1,)), ((), ())), preferred_element_type=F32)


def _mem_kv_kernel(mem_ref, g_ref, wk_ref, wv_ref, mk_ref, mv_ref):
    mn = _rms(mem_ref[0], g_ref[0]).astype(BF16)
    mk_ref[0, 0] = _dot(mn, wk_ref[0]).astype(BF16)
    mv_ref[0, 0] = _dot(mn, wv_ref[0]).astype(BF16)


def _mem_kv(mem, g_mem, w_k, w_v):
    nb, ml, d = mem.shape
    nl, _, mw = w_k.shape
    out = jax.ShapeDtypeStruct((nl, nb, ml, mw), BF16)
    return pl.pallas_call(
        _mem_kv_kernel,
        grid=(nl, nb),
        in_specs=[
            pl.BlockSpec((1, ml, d), lambda l, b: (b, 0, 0)),
            pl.BlockSpec((1, 1, d), lambda l, b: (l, 0, 0)),
            pl.BlockSpec((1, d, mw), lambda l, b: (l, 0, 0)),
            pl.BlockSpec((1, d, mw), lambda l, b: (l, 0, 0)),
        ],
        out_specs=[
            pl.BlockSpec((1, 1, ml, mw), lambda l, b: (l, b, 0, 0)),
            pl.BlockSpec((1, 1, ml, mw), lambda l, b: (l, b, 0, 0)),
        ],
        out_shape=[out, out],
        compiler_params=_params(("arbitrary", "arbitrary")),
        name="mem_kv",
    )(mem, g_mem.reshape(nl, 1, d), w_k, w_v)


def _norm_matmul_kernel(x_ref, g_ref, w_ref, o_ref, xn_ref):
    @pl.when(pl.program_id(1) == 0)
    def _():
        xn_ref[...] = _rms(x_ref[...], g_ref[...]).astype(BF16)

    o_ref[...] = _dot(xn_ref[...], w_ref[...])


def _norm_matmul(x, g, w, tm, tn):
    m, k = x.shape
    n = w.shape[1]
    return pl.pallas_call(
        _norm_matmul_kernel,
        grid=(m // tm, n // tn),
        in_specs=[
            pl.BlockSpec((tm, k), lambda i, j: (i, 0)),
            pl.BlockSpec((1, k), lambda i, j: (0, 0)),
            pl.BlockSpec((k, tn), lambda i, j: (0, j)),
        ],
        out_specs=pl.BlockSpec((tm, tn), lambda i, j: (i, j)),
        out_shape=jax.ShapeDtypeStruct((m, n), F32),
        scratch_shapes=[pltpu.VMEM((tm, k), BF16)],
        compiler_params=_params(("arbitrary", "arbitrary")),
        name="norm_matmul",
    )(x, g.reshape(1, k), w)


def _rglru_kernel(xb_ref, gate_ref, cw_ref, cb_ref, wr_ref, br_ref, wi_ref, bi_ref, lam_ref,
                  o_ref, xpad_ref, a_ref, b_ref, carry_ref, *, ts, conv_width, n_blocks):
    c = xb_ref.shape[1]
    pad = SUBLANES

    @pl.when(pl.program_id(1) == 0)
    def _():
        xpad_ref[0:pad, :] = jnp.zeros((pad, c), F32)
        carry_ref[...] = jnp.zeros_like(carry_ref)

    @pl.when(pl.program_id(1) != 0)
    def _():
        xpad_ref[0:pad, :] = xpad_ref[ts:ts + pad, :]

    xpad_ref[pad:pad + ts, :] = xb_ref[...]

    xc = cb_ref[...] + jnp.zeros((ts, c), F32)
    for k in range(conv_width):
        off = pad - (conv_width - 1) + k
        xc = xc + cw_ref[k:k + 1, :] * xpad_ref[off:off + ts, :]

    sp = jax.nn.softplus(-lam_ref[...])
    for n in range(n_blocks):
        sl = slice(n * HEAD_DIM, (n + 1) * HEAD_DIM)
        xn = xc[:, sl]
        xn16 = xn.astype(BF16)
        r = jax.nn.sigmoid(_dot(xn16, wr_ref[n]) + br_ref[:, sl])
        i = jax.nn.sigmoid(_dot(xn16, wi_ref[n]) + bi_ref[:, sl])
        log_a = (-LRU_C * r) * sp[:, sl]
        a = jnp.exp(log_a)
        a_ref[:, sl] = a
        b_ref[:, sl] = jnp.sqrt(jnp.tanh(-log_a) * (1.0 + a * a)) * (i * xn)

    row = lax.broadcasted_iota(jnp.int32, (SUBLANES, c), 0)

    def body(g, carry):
        r0 = pl.multiple_of(g * SUBLANES, SUBLANES)
        a = a_ref[pl.ds(r0, SUBLANES), :]
        b = b_ref[pl.ds(r0, SUBLANES), :]
        for d in (1, 2, 4):
            keep = row >= d
            a_sh = pltpu.roll(a, d, 0)
            b_sh = pltpu.roll(b, d, 0)
            b = jnp.where(keep, a * b_sh + b, b)
            a = jnp.where(keep, a * a_sh, a)
        h = b + a * carry
        b_ref[pl.ds(r0, SUBLANES), :] = h
        return jnp.broadcast_to(h[SUBLANES - 1:SUBLANES, :], (SUBLANES, c))

    carry_ref[...] = lax.fori_loop(0, ts // SUBLANES, body, carry_ref[...])

    o_ref[...] = (b_ref[...] * jax.nn.gelu(gate_ref[...])).astype(BF16)


def _rglru(proj, nb, seq, cw, cb, wr, br, wi, bi, lam, ts):
    c = cw.shape[1]
    conv_width = cw.shape[0]
    n_blocks = wr.shape[0]
    spb = seq // ts
    row = lambda i: pl.BlockSpec((1, c), lambda b, s: (0, 0))
    gates = pl.BlockSpec((n_blocks, HEAD_DIM, HEAD_DIM), lambda b, s: (0, 0, 0))
    kern = functools.partial(_rglru_kernel, ts=ts, conv_width=conv_width, n_blocks=n_blocks)
    return pl.pallas_call(
        kern,
        grid=(nb, spb),
        in_specs=[
            pl.BlockSpec((ts, c), lambda b, s: (b * spb + s, 0)),
            pl.BlockSpec((ts, c), lambda b, s: (b * spb + s, 1)),
            pl.BlockSpec((conv_width, c), lambda b, s: (0, 0)),
            row(0), gates, row(0), gates, row(0), row(0),
        ],
        out_specs=pl.BlockSpec((ts, c), lambda b, s: (b * spb + s, 0)),
        out_shape=jax.ShapeDtypeStruct((nb * seq, c), BF16),
        scratch_shapes=[
            pltpu.VMEM((ts + SUBLANES, c), F32),
            pltpu.VMEM((ts, c), F32),
            pltpu.VMEM((ts, c), F32),
            pltpu.VMEM((SUBLANES, c), F32),
        ],
        compiler_params=_params(("arbitrary", "arbitrary")),
        name="rglru",
    )(proj, proj, cw, cb.reshape(1, c), wr, br.reshape(1, c), wi, bi.reshape(1, c), lam.reshape(1, c))


def _mix_out_kernel(tok_ref, qm_ref, mk_ref, mv_ref, wo_ref, h_ref, g_ref, o_ref, *, tok_width):
    scale = HEAD_DIM ** -0.5
    y = _dot(tok_ref[...], wo_ref[0:tok_width, :])
    for hh in range(MEM_HEADS):
        sl = slice(hh * HEAD_DIM, (hh + 1) * HEAD_DIM)
        q = (qm_ref[:, sl] * scale).astype(BF16)
        s = _dot_nt(q, mk_ref[0, 0, :, sl])
        p = jnp.exp(s - jnp.max(s, axis=-1, keepdims=True))
        l = jnp.sum(p, axis=-1, keepdims=True)
        o = _dot(p.astype(BF16), mv_ref[0, 0, :, sl]) / l
        y = y + _dot(o.astype(BF16), wo_ref[tok_width + hh * HEAD_DIM:tok_width + (hh + 1) * HEAD_DIM, :])
    o_ref[...] = h_ref[...] + _rms(y, g_ref[...])


def _mix_out(tok, qm_src, qm_col, mk, mv, layer, w_o, h, g_post, seq, tm):
    t, tok_width = tok.shape
    d = h.shape[1]
    ml, mw = mk.shape[2], mk.shape[3]
    bpb = seq // tm
    kern = functools.partial(_mix_out_kernel, tok_width=tok_width)
    return pl.pallas_call(
        kern,
        grid=(t // tm,),
        in_specs=[
            pl.BlockSpec((tm, tok_width), lambda i: (i, 0)),
            pl.BlockSpec((tm, mw), lambda i: (i, qm_col)),
            pl.BlockSpec((1, 1, ml, mw), lambda i: (layer, i // bpb, 0, 0)),
            pl.BlockSpec((1, 1, ml, mw), lambda i: (layer, i // bpb, 0, 0)),
            pl.BlockSpec((tok_width + mw, d), lambda i: (0, 0)),
            pl.BlockSpec((tm, d), lambda i: (i, 0)),
            pl.BlockSpec((1, d), lambda i: (0, 0)),
        ],
        out_specs=pl.BlockSpec((tm, d), lambda i: (i, 0)),
        out_shape=jax.ShapeDtypeStruct((t, d), F32),
        compiler_params=_params(("arbitrary",)),
        name="mix_out",
    )(tok, qm_src, mk, mv, w_o, h, g_post.reshape(1, d))


def _mlp_kernel(h_ref, gpre_ref, w1_ref, w2_ref, gpost_ref, o_ref, hn_ref, acc_ref):
    j = pl.program_id(1)

    @pl.when(j == 0)
    def _():
        hn_ref[...] = _rms(h_ref[...], gpre_ref[...]).astype(BF16)
        acc_ref[...] = jnp.zeros_like(acc_ref)

    f = _dot(hn_ref[...], w1_ref[...])
    f = jnp.square(jnp.maximum(f, 0.0)).astype(BF16)
    acc_ref[...] += _dot(f, w2_ref[...])

    @pl.when(j == pl.num_programs(1) - 1)
    def _():
        o_ref[...] = h_ref[...] + _rms(acc_ref[...], gpost_ref[...])


def _mlp(h, g_pre, w1, w2, g_post, tm, tf):
    t, d = h.shape
    dff = w1.shape[1]
    return pl.pallas_call(
        _mlp_kernel,
        grid=(t // tm, dff // tf),
        in_specs=[
            pl.BlockSpec((tm, d), lambda i, j: (i, 0)),
            pl.BlockSpec((1, d), lambda i, j: (0, 0)),
            pl.BlockSpec((d, tf), lambda i, j: (0, j)),
            pl.BlockSpec((tf, d), lambda i, j: (j, 0)),
            pl.BlockSpec((1, d), lambda i, j: (0, 0)),
        ],
        out_specs=pl.BlockSpec((tm, d), lambda i, j: (i, 0)),
        out_shape=jax.ShapeDtypeStruct((t, d), F32),
        scratch_shapes=[pltpu.VMEM((tm, d), BF16), pltpu.VMEM((tm, d), F32)],
        compiler_params=_params(("arbitrary", "arbitrary")),
        name="mlp",
    )(h, g_pre.reshape(1, d), w1, w2, g_post.reshape(1, d))


def _swap_halves(x, lane):
    return jnp.where(lane % ROPE_DIM < ROPE_DIM // 2, pltpu.roll(x, 128 - ROPE_DIM // 2, 1),
                     pltpu.roll(x, ROPE_DIM // 2, 1))


def _proj_b_kernel(h_ref, pos_ref, freq_ref, gkv_ref, gmix_ref, wdown_ref, glat_ref, wupk_ref, wupv_ref,
                   win_ref, gqa_ref, wqn_ref, wqp_ref,
                   q_ref, k_ref, v_ref, qm_ref, *, n_heads, kv_rank, q_rank):
    h = h_ref[...]
    tm = h.shape[0]
    xn = h * lax.rsqrt(jnp.mean(h * h, axis=-1, keepdims=True) + EPS)

    lane = lax.broadcasted_iota(jnp.int32, (tm, 128), 1)
    ang = pos_ref[...].astype(F32) * freq_ref[...]
    cos = jnp.cos(ang)
    sin = jnp.where(lane % ROPE_DIM < ROPE_DIM // 2, -jnp.sin(ang), jnp.sin(ang))
    low = lane < ROPE_DIM

    def rope(x):
        return x * cos + _swap_halves(x, lane) * sin

    ckv = _dot((xn * gkv_ref[...]).astype(BF16), wdown_ref[...])
    latent = _rms(ckv[:, :kv_rank], glat_ref[...]).astype(BF16)
    kpe_raw = jnp.concatenate([ckv[:, kv_rank:], jnp.zeros((tm, 128 - ROPE_DIM), F32)], axis=1)
    kpe = jnp.where(low, rope(kpe_raw), 0.0).astype(BF16)
    k_nope = _dot(latent, wupk_ref[...])
    v_ref[...] = _dot(latent, wupv_ref[...]).astype(BF16)
    for hh in range(n_heads):
        k_ref[:, hh * QK_PAD:hh * QK_PAD + HEAD_DIM] = k_nope[:, hh * HEAD_DIM:(hh + 1) * HEAD_DIM].astype(BF16)
        k_ref[:, hh * QK_PAD + HEAD_DIM:(hh + 1) * QK_PAD] = kpe

    proj = _dot((xn * gmix_ref[...]).astype(BF16), win_ref[...])
    qm_ref[...] = proj[:, q_rank:]
    cq = _rms(proj[:, :q_rank], gqa_ref[...]).astype(BF16)
    scale = QK_HEAD_DIM ** -0.5
    q_nope = _dot(cq, wqn_ref[...]) * scale
    q_pe = _dot(cq, wqp_ref[...]) * scale
    for hp in range(n_heads // 2):
        pe = rope(q_pe[:, hp * 128:(hp + 1) * 128])
        for e in range(2):
            hh = 2 * hp + e
            q_ref[:, hh * QK_PAD:hh * QK_PAD + HEAD_DIM] = q_nope[:, hh * HEAD_DIM:(hh + 1) * HEAD_DIM].astype(BF16)
            pe_h = pe if e == 0 else pltpu.roll(pe, ROPE_DIM, 1)
            q_ref[:, hh * QK_PAD + HEAD_DIM:(hh + 1) * QK_PAD] = jnp.where(low, pe_h, 0.0).astype(BF16)


def _proj_b(h, pos, freq, g_kv, g_mix, w_down, g_lat, w_upk, w_upv, w_in, g_qa, w_qn, w_qp, tm):
    t, d = h.shape
    kv_rank = w_upk.shape[0]
    q_rank = w_qn.shape[0]
    n_heads = w_upk.shape[1] // HEAD_DIM
    mw = w_in.shape[1] - q_rank
    full = lambda a: pl.BlockSpec(a.shape, lambda i: (0,) * a.ndim)
    g_kv, g_mix, g_lat, g_qa = (g_kv.reshape(1, -1), g_mix.reshape(1, -1), g_lat.reshape(1, -1),
                                g_qa.reshape(1, -1))
    kern = functools.partial(_proj_b_kernel, n_heads=n_heads, kv_rank=kv_rank, q_rank=q_rank)
    return pl.pallas_call(
        kern,
        grid=(t // tm,),
        in_specs=[
            pl.BlockSpec((tm, d), lambda i: (i, 0)),
            pl.BlockSpec((tm, 1), lambda i: (i, 0)),
            full(freq), full(g_kv), full(g_mix), full(w_down), full(g_lat), full(w_upk), full(w_upv),
            full(w_in), full(g_qa), full(w_qn), full(w_qp),
        ],
        out_specs=[
            pl.BlockSpec((tm, n_heads * QK_PAD), lambda i: (i, 0)),
            pl.BlockSpec((tm, n_heads * QK_PAD), lambda i: (i, 0)),
            pl.BlockSpec((tm, n_heads * HEAD_DIM), lambda i: (i, 0)),
            pl.BlockSpec((tm, mw), lambda i: (i, 0)),
        ],
        out_shape=[
            jax.ShapeDtypeStruct((t, n_heads * QK_PAD), BF16),
            jax.ShapeDtypeStruct((t, n_heads * QK_PAD), BF16),
            jax.ShapeDtypeStruct((t, n_heads * HEAD_DIM), BF16),
            jax.ShapeDtypeStruct((t, mw), F32),
        ],
        compiler_params=_params(("arbitrary",)),
        name="proj_b",
    )(h, pos, freq, g_kv, g_mix, w_down, g_lat, w_upk, w_upv, w_in, g_qa, w_qn, w_qp)


def _mla_kernel(q_ref, k_ref, v_ref, o_ref, m_ref, l_ref, acc_ref, *, tq):
    qi = pl.program_id(2)
    q = q_ref[...]
    m_ref[...] = jnp.full_like(m_ref, -1e30)
    l_ref[...] = jnp.zeros_like(l_ref)
    acc_ref[...] = jnp.zeros_like(acc_ref)

    def step(kb, masked):
        r0 = pl.multiple_of(kb * tq, tq)
        s = _dot_nt(q, k_ref[pl.ds(r0, tq), :])
        if masked:
            qc = lax.broadcasted_iota(jnp.int32, (tq, tq), 0) // CHUNK
            kc = lax.broadcasted_iota(jnp.int32, (tq, tq), 1) // CHUNK
            s = jnp.where(kc <= qc, s, -1e30)
        m_old = m_ref[...]
        m_new = jnp.maximum(m_old, jnp.max(s, axis=-1, keepdims=True))
        alpha = jnp.exp(m_old - m_new)
        p = jnp.exp(s - m_new)
        l_ref[...] = alpha * l_ref[...] + jnp.sum(p, axis=-1, keepdims=True)
        acc_ref[...] = alpha * acc_ref[...] + _dot(p.astype(BF16), v_ref[pl.ds(r0, tq), :])
        m_ref[...] = m_new

    def body(kb, c):
        step(kb, False)
        return c

    lax.fori_loop(0, qi, body, 0)
    step(qi, True)
    o_ref[...] = (acc_ref[...] / l_ref[...]).astype(BF16)


def _mla(q, k, v, nb, seq, n_heads, tq):
    t = q.shape[0]
    nq = seq // tq
    kern = functools.partial(_mla_kernel, tq=tq)
    return pl.pallas_call(
        kern,
        grid=(nb, n_heads, nq),
        in_specs=[
            pl.BlockSpec((tq, QK_PAD), lambda b, h, i: (b * nq + i, h)),
            pl.BlockSpec((seq, QK_PAD), lambda b, h, i: (b, h)),
            pl.BlockSpec((seq, HEAD_DIM), lambda b, h, i: (b, h)),
        ],
        out_specs=pl.BlockSpec((tq, HEAD_DIM), lambda b, h, i: (b * nq + i, h)),
        out_shape=jax.ShapeDtypeStruct((t, n_heads * HEAD_DIM), BF16),
        scratch_shapes=[
            pltpu.VMEM((tq, 1), F32),
            pltpu.VMEM((tq, 1), F32),
            pltpu.VMEM((tq, HEAD_DIM), F32),
        ],
        compiler_params=_params(("arbitrary", "arbitrary", "arbitrary")),
        name="mla",
    )(q, k, v)


def kernel(x, mem, positions, g_mix_pre, g_mix_post, g_mlp_pre, g_mlp_post, g_mem, w_mem_k, w_mem_v, w_o, w_ff1, w_ff2, a_w_in, a_conv_w, a_conv_b, a_w_rgate, a_b_rgate, a_w_igate, a_b_igate, a_lambda, b_w_in, b_g_qa, b_w_qb, kv_g_in, kv_w_down, kv_g_latent, kv_w_up):
    nb, seq, d = x.shape
    t = nb * seq
    depth = g_mix_pre.shape[0]
    n_a = a_w_in.shape[0]
    lru_width = a_conv_w.shape[2]
    n_heads = lru_width // HEAD_DIM
    kv_rank = kv_g_latent.shape[0]
    q_rank = b_g_qa.shape[1]
    assert kv_w_down.shape[1] == kv_rank + ROPE_DIM
    assert seq % 512 == 0 and d % 128 == 0
    assert depth - n_a == 1, "the shared K/V projection is fused into the single MLA layer's projections"

    bf = lambda w: w.astype(BF16)
    mk, mv = _mem_kv(mem, g_mem, bf(w_mem_k), bf(w_mem_v))

    half = ROPE_DIM // 2
    inv_freq = ROPE_THETA ** (-jnp.arange(half, dtype=F32) / half)
    freq = jnp.tile(inv_freq, 128 // half).reshape(1, 128)
    pos = positions.reshape(t, 1)

    w_up = bf(kv_w_up).reshape(kv_rank, n_heads, 2, HEAD_DIM)
    w_upk = w_up[:, :, 0, :].reshape(kv_rank, n_heads * HEAD_DIM)
    w_upv = w_up[:, :, 1, :].reshape(kv_rank, n_heads * HEAD_DIM)

    h = x.reshape(t, d)
    for layer in range(depth):
        if layer < n_a:
            la = layer
            proj = _norm_matmul(h, g_mix_pre[layer], bf(a_w_in[la]), tm=512, tn=512)
            tok = _rglru(proj, nb, seq, a_conv_w[la], a_conv_b[la], bf(a_w_rgate[la]), a_b_rgate[la],
                         bf(a_w_igate[la]), a_b_igate[la], a_lambda[la], ts=256)
            qm_src, qm_col = proj, (2 * lru_width) // (MEM_HEADS * HEAD_DIM)
        else:
            lb = layer - n_a
            w_qb = bf(b_w_qb[lb]).reshape(q_rank, n_heads, QK_HEAD_DIM)
            w_qn = w_qb[:, :, :HEAD_DIM].reshape(q_rank, n_heads * HEAD_DIM)
            w_qp = w_qb[:, :, HEAD_DIM:].reshape(q_rank, n_heads * ROPE_DIM)
            q, k, v, qm_src = _proj_b(h, pos, freq, kv_g_in, g_mix_pre[layer], bf(kv_w_down), kv_g_latent,
                                      w_upk, w_upv, bf(b_w_in[lb]), b_g_qa[lb], w_qn, w_qp, tm=256)
            qm_col = 0
            tok = _mla(q, k, v, nb, seq, n_heads, tq=512)
        h = _mix_out(tok, qm_src, qm_col, mk, mv, layer, bf(w_o[layer]), h, g_mix_post[layer], seq, tm=256)
        h = _mlp(h, g_mlp_pre[layer], bf(w_ff1[layer]), bf(w_ff2[layer]), g_mlp_post[layer], tm=512, tf=512)
    return h.reshape(nb, seq, d)
```

```python
import functools

import jax
import jax.numpy as jnp
from jax import lax
from jax.experimental import pallas as pl
from jax.experimental.pallas import tpu as pltpu

EPS = 1e-6
LRU_C = 8.0
CHUNK = 64
ROPE_THETA = 10000.0
LOG2_E = 1.4426950408889634
MEM_HEADS = 4
HEAD_DIM = 128
ROPE_DIM = 64
QK_HEAD_DIM = HEAD_DIM + ROPE_DIM
QK_PAD = 256
SUBLANES = 8
VMEM_LIMIT = 56 * 1024 * 1024

BF16 = jnp.bfloat16
F32 = jnp.float32


def _params(sem, vmem=VMEM_LIMIT):
    return pltpu.CompilerParams(dimension_semantics=sem, vmem_limit_bytes=vmem)


def _rms(x, g):
    return x * lax.rsqrt(jnp.mean(x * x, axis=-1, keepdims=True) + EPS) * g


def _dot(a, b):
    return jnp.dot(a, b, preferred_element_type=F32)


def _dot_nt(a, b):
    return lax.dot_general(a, b, (((1,), (1,)), ((), ())), preferred_element_type=F32)


def _mem_kv_kernel(mem_ref, g_ref, wk_ref, wv_ref, mk_ref, mv_ref):
    mn = _rms(mem_ref[0], g_ref[0]).astype(BF16)
    mk_ref[0, 0] = _dot(mn, wk_ref[0]).astype(BF16)
    mv = _dot(mn, wv_ref[0]).astype(BF16)
    ones = jnp.ones((mv.shape[0], HEAD_DIM), BF16)
    for hh in range(MEM_HEADS):
        mv_ref[0, 0, :, 2 * hh * HEAD_DIM:(2 * hh + 1) * HEAD_DIM] = mv[:, hh * HEAD_DIM:(hh + 1) * HEAD_DIM]
        mv_ref[0, 0, :, (2 * hh + 1) * HEAD_DIM:(2 * hh + 2) * HEAD_DIM] = ones


def _mem_kv(mem, g_mem, w_k, w_v):
    nb, ml, d = mem.shape
    nl, _, mw = w_k.shape
    assert mw == MEM_HEADS * HEAD_DIM
    return pl.pallas_call(
        _mem_kv_kernel,
        grid=(nl, nb),
        in_specs=[
            pl.BlockSpec((1, ml, d), lambda l, b: (b, 0, 0)),
            pl.BlockSpec((1, 1, d), lambda l, b: (l, 0, 0)),
            pl.BlockSpec((1, d, mw), lambda l, b: (l, 0, 0)),
            pl.BlockSpec((1, d, mw), lambda l, b: (l, 0, 0)),
        ],
        out_specs=[
            pl.BlockSpec((1, 1, ml, mw), lambda l, b: (l, b, 0, 0)),
            pl.BlockSpec((1, 1, ml, 2 * mw), lambda l, b: (l, b, 0, 0)),
        ],
        out_shape=[jax.ShapeDtypeStruct((nl, nb, ml, mw), BF16),
                   jax.ShapeDtypeStruct((nl, nb, ml, 2 * mw), BF16)],
        compiler_params=_params(("arbitrary", "arbitrary")),
        name="mem_kv",
    )(mem, g_mem.reshape(nl, 1, d), w_k, w_v)


def _norm_matmul_kernel(x_ref, g_ref, w_ref, o_ref, xn_ref):
    @pl.when(pl.program_id(1) == 0)
    def _():
        xn_ref[...] = _rms(x_ref[...], g_ref[...]).astype(BF16)

    o_ref[...] = _dot(xn_ref[...], w_ref[...])


def _norm_matmul(x, g, w, tm, tn):
    m, k = x.shape
    n = w.shape[1]
    return pl.pallas_call(
        _norm_matmul_kernel,
        grid=(m // tm, n // tn),
        in_specs=[
            pl.BlockSpec((tm, k), lambda i, j: (i, 0)),
            pl.BlockSpec((1, k), lambda i, j: (0, 0)),
            pl.BlockSpec((k, tn), lambda i, j: (0, j)),
        ],
        out_specs=pl.BlockSpec((tm, tn), lambda i, j: (i, j)),
        out_shape=jax.ShapeDtypeStruct((m, n), F32),
        scratch_shapes=[pltpu.VMEM((tm, k), BF16)],
        compiler_params=_params(("arbitrary", "arbitrary")),
        name="norm_matmul",
    )(x, g.reshape(1, k), w)


def _rglru_kernel(xb_ref, gate_ref, cw_ref, cb_ref, wr_ref, br_ref, wi_ref, bi_ref, lam_ref,
                  o_ref, xpad_ref, a_ref, b_ref, carry_ref, *, ts, conv_width, n_blocks):
    c = xb_ref.shape[1]
    pad = SUBLANES

    @pl.when(pl.program_id(1) == 0)
    def _():
        xpad_ref[0:pad, :] = jnp.zeros((pad, c), F32)
        carry_ref[...] = jnp.zeros_like(carry_ref)

    @pl.when(pl.program_id(1) != 0)
    def _():
        xpad_ref[0:pad, :] = xpad_ref[ts:ts + pad, :]

    xpad_ref[pad:pad + ts, :] = xb_ref[...]

    xc = cb_ref[...] + jnp.zeros((ts, c), F32)
    for k in range(conv_width):
        off = pad - (conv_width - 1) + k
        xc = xc + cw_ref[k:k + 1, :] * xpad_ref[off:off + ts, :]

    sp = jax.nn.softplus(-lam_ref[...])
    for n in range(n_blocks):
        sl = slice(n * HEAD_DIM, (n + 1) * HEAD_DIM)
        xn = xc[:, sl]
        xn16 = xn.astype(BF16)
        r = jax.nn.sigmoid(_dot(xn16, wr_ref[n]) + br_ref[:, sl])
        i = jax.nn.sigmoid(_dot(xn16, wi_ref[n]) + bi_ref[:, sl])
        log_a = (-LRU_C * r) * sp[:, sl]
        a = jnp.exp(log_a)
        a_ref[:, sl] = a
        b_ref[:, sl] = jnp.sqrt(jnp.tanh(-log_a) * (1.0 + a * a)) * (i * xn)

    row = lax.broadcasted_iota(jnp.int32, (SUBLANES, c), 0)

    def body(g, carry):
        r0 = pl.multiple_of(g * SUBLANES, SUBLANES)
        a = a_ref[pl.ds(r0, SUBLANES), :]
        b = b_ref[pl.ds(r0, SUBLANES), :]
        for d in (1, 2, 4):
            keep = row >= d
            a_sh = pltpu.roll(a, d, 0)
            b_sh = pltpu.roll(b, d, 0)
            b = jnp.where(keep, a * b_sh + b, b)
            a = jnp.where(keep, a * a_sh, a)
        h = b + a * carry
        b_ref[pl.ds(r0, SUBLANES), :] = h
        return jnp.broadcast_to(h[SUBLANES - 1:SUBLANES, :], (SUBLANES, c))

    carry_ref[...] = lax.fori_loop(0, ts // SUBLANES, body, carry_ref[...])

    o_ref[...] = (b_ref[...] * jax.nn.gelu(gate_ref[...])).astype(BF16)


def _rglru(proj, nb, seq, cw, cb, wr, br, wi, bi, lam, ts):
    c = cw.shape[1]
    conv_width = cw.shape[0]
    n_blocks = wr.shape[0]
    spb = seq // ts
    row = lambda i: pl.BlockSpec((1, c), lambda b, s: (0, 0))
    gates = pl.BlockSpec((n_blocks, HEAD_DIM, HEAD_DIM), lambda b, s: (0, 0, 0))
    kern = functools.partial(_rglru_kernel, ts=ts, conv_width=conv_width, n_blocks=n_blocks)
    return pl.pallas_call(
        kern,
        grid=(nb, spb),
        in_specs=[
            pl.BlockSpec((ts, c), lambda b, s: (b * spb + s, 0)),
            pl.BlockSpec((ts, c), lambda b, s: (b * spb + s, 1)),
            pl.BlockSpec((conv_width, c), lambda b, s: (0, 0)),
            row(0), gates, row(0), gates, row(0), row(0),
        ],
        out_specs=pl.BlockSpec((ts, c), lambda b, s: (b * spb + s, 0)),
        out_shape=jax.ShapeDtypeStruct((nb * seq, c), BF16),
        scratch_shapes=[
            pltpu.VMEM((ts + SUBLANES, c), F32),
            pltpu.VMEM((ts, c), F32),
            pltpu.VMEM((ts, c), F32),
            pltpu.VMEM((SUBLANES, c), F32),
        ],
        compiler_params=_params(("arbitrary", "arbitrary")),
        name="rglru",
    )(proj, proj, cw, cb.reshape(1, c), wr, br.reshape(1, c), wi, bi.reshape(1, c), lam.reshape(1, c))


def _mix_out_kernel(tok_ref, qm_ref, mk_ref, mv_ref, wo_ref, h_ref, g_ref, o_ref, *, tok_width):
    scale = LOG2_E * HEAD_DIM ** -0.5
    y = _dot(tok_ref[...], wo_ref[0:tok_width, :])
    heads = []
    for hh in range(MEM_HEADS):
        sl = slice(hh * HEAD_DIM, (hh + 1) * HEAD_DIM)
        q = (qm_ref[:, sl] * scale).astype(BF16)
        s = _dot_nt(q, mk_ref[0, 0, :, sl])
        p = jnp.exp2(s - jnp.max(s, axis=-1, keepdims=True))
        pv = _dot(p.astype(BF16), mv_ref[0, 0, :, 2 * hh * HEAD_DIM:(2 * hh + 2) * HEAD_DIM])
        heads.append((pv[:, :HEAD_DIM] / pv[:, HEAD_DIM:]).astype(BF16))
    y = y + _dot(jnp.concatenate(heads, axis=1), wo_ref[tok_width:, :])
    o_ref[...] = h_ref[...] + _rms(y, g_ref[...])


def _mix_out(tok, qm_src, qm_col, mk, mv, layer, w_o, h, g_post, seq, tm):
    t, tok_width = tok.shape
    d = h.shape[1]
    ml, mw = mk.shape[2], mk.shape[3]
    bpb = seq // tm
    kern = functools.partial(_mix_out_kernel, tok_width=tok_width)
    return pl.pallas_call(
        kern,
        grid=(t // tm,),
        in_specs=[
            pl.BlockSpec((tm, tok_width), lambda i: (i, 0)),
            pl.BlockSpec((tm, mw), lambda i: (i, qm_col)),
            pl.BlockSpec((1, 1, ml, mw), lambda i: (layer, i // bpb, 0, 0)),
            pl.BlockSpec((1, 1, ml, 2 * mw), lambda i: (layer, i // bpb, 0, 0)),
            pl.BlockSpec((tok_width + mw, d), lambda i: (0, 0)),
            pl.BlockSpec((tm, d), lambda i: (i, 0)),
            pl.BlockSpec((1, d), lambda i: (0, 0)),
        ],
        out_specs=pl.BlockSpec((tm, d), lambda i: (i, 0)),
        out_shape=jax.ShapeDtypeStruct((t, d), F32),
        compiler_params=_params(("arbitrary",)),
        name="mix_out",
    )(tok, qm_src, mk, mv, w_o, h, g_post.reshape(1, d))


def _mlp_kernel(h_ref, gpre_ref, w1_ref, w2_ref, gpost_ref, o_ref, hn_ref):
    j = pl.program_id(1)

    @pl.when(j == 0)
    def _():
        hn_ref[...] = _rms(h_ref[...], gpre_ref[...]).astype(BF16)
        o_ref[...] = jnp.zeros_like(o_ref)

    f = _dot(hn_ref[...], w1_ref[...])
    f = jnp.square(jnp.maximum(f, 0.0)).astype(BF16)
    o_ref[...] += _dot(f, w2_ref[...])

    @pl.when(j == pl.num_programs(1) - 1)
    def _():
        o_ref[...] = h_ref[...] + _rms(o_ref[...], gpost_ref[...])


def _mlp(h, g_pre, w1, w2, g_post, tm, tf):
    t, d = h.shape
    dff = w1.shape[1]
    return pl.pallas_call(
        _mlp_kernel,
        grid=(t // tm, dff // tf),
        in_specs=[
            pl.BlockSpec((tm, d), lambda i, j: (i, 0)),
            pl.BlockSpec((1, d), lambda i, j: (0, 0)),
            pl.BlockSpec((d, tf), lambda i, j: (0, j)),
            pl.BlockSpec((tf, d), lambda i, j: (j, 0)),
            pl.BlockSpec((1, d), lambda i, j: (0, 0)),
        ],
        out_specs=pl.BlockSpec((tm, d), lambda i, j: (i, 0)),
        out_shape=jax.ShapeDtypeStruct((t, d), F32),
        scratch_shapes=[pltpu.VMEM((tm, d), BF16)],
        compiler_params=_params(("arbitrary", "arbitrary")),
        name="mlp",
    )(h, g_pre.reshape(1, d), w1, w2, g_post.reshape(1, d))


def _swap_halves(x, lane):
    return jnp.where(lane % ROPE_DIM < ROPE_DIM // 2, pltpu.roll(x, 128 - ROPE_DIM // 2, 1),
                     pltpu.roll(x, ROPE_DIM // 2, 1))


def _proj_b_kernel(h_ref, pos_ref, freq_ref, gkv_ref, gmix_ref, wdown_ref, glat_ref, wupk_ref, wupv_ref,
                   win_ref, gqa_ref, wqn_ref, wqp_ref,
                   q_ref, k_ref, v_ref, qm_ref, *, n_heads, kv_rank, q_rank):
    h = h_ref[...]
    tm = h.shape[0]
    xn = h * lax.rsqrt(jnp.mean(h * h, axis=-1, keepdims=True) + EPS)

    lane = lax.broadcasted_iota(jnp.int32, (tm, 128), 1)
    ang = pos_ref[...].astype(F32) * freq_ref[...]
    cos = jnp.cos(ang)
    sin = jnp.where(lane % ROPE_DIM < ROPE_DIM // 2, -jnp.sin(ang), jnp.sin(ang))
    low = lane < ROPE_DIM

    def rope(x):
        return x * cos + _swap_halves(x, lane) * sin

    ckv = _dot((xn * gkv_ref[...]).astype(BF16), wdown_ref[...])
    latent = _rms(ckv[:, :kv_rank], glat_ref[...]).astype(BF16)
    kpe_raw = jnp.concatenate([ckv[:, kv_rank:], jnp.zeros((tm, 128 - ROPE_DIM), F32)], axis=1)
    kpe = jnp.where(low, rope(kpe_raw), 0.0).astype(BF16)
    k_nope = _dot(latent, wupk_ref[...])
    v = _dot(latent, wupv_ref[...])
    ones = jnp.ones((tm, HEAD_DIM), BF16)
    for hh in range(n_heads):
        sl = slice(hh * HEAD_DIM, (hh + 1) * HEAD_DIM)
        k_ref[:, hh * QK_PAD:hh * QK_PAD + HEAD_DIM] = k_nope[:, sl].astype(BF16)
        k_ref[:, hh * QK_PAD + HEAD_DIM:(hh + 1) * QK_PAD] = kpe
        v_ref[:, 2 * hh * HEAD_DIM:(2 * hh + 1) * HEAD_DIM] = v[:, sl].astype(BF16)
        v_ref[:, (2 * hh + 1) * HEAD_DIM:(2 * hh + 2) * HEAD_DIM] = ones

    proj = _dot((xn * gmix_ref[...]).astype(BF16), win_ref[...])
    qm_ref[...] = proj[:, q_rank:]
    cq = _rms(proj[:, :q_rank], gqa_ref[...]).astype(BF16)
    scale = LOG2_E * QK_HEAD_DIM ** -0.5
    q_nope = _dot(cq, wqn_ref[...]) * scale
    q_pe = _dot(cq, wqp_ref[...]) * scale
    for hp in range(n_heads // 2):
        pe = rope(q_pe[:, hp * 128:(hp + 1) * 128])
        for e in range(2):
            hh = 2 * hp + e
            q_ref[:, hh * QK_PAD:hh * QK_PAD + HEAD_DIM] = q_nope[:, hh * HEAD_DIM:(hh + 1) * HEAD_DIM].astype(BF16)
            pe_h = pe if e == 0 else pltpu.roll(pe, ROPE_DIM, 1)
            q_ref[:, hh * QK_PAD + HEAD_DIM:(hh + 1) * QK_PAD] = jnp.where(low, pe_h, 0.0).astype(BF16)


def _proj_b(h, pos, freq, g_kv, g_mix, w_down, g_lat, w_upk, w_upv, w_in, g_qa, w_qn, w_qp, tm):
    t, d = h.shape
    kv_rank = w_upk.shape[0]
    q_rank = w_qn.shape[0]
    n_heads = w_upk.shape[1] // HEAD_DIM
    mw = w_in.shape[1] - q_rank
    full = lambda a: pl.BlockSpec(a.shape, lambda i: (0,) * a.ndim)
    g_kv, g_mix, g_lat, g_qa = (g_kv.reshape(1, -1), g_mix.reshape(1, -1), g_lat.reshape(1, -1),
                                g_qa.reshape(1, -1))
    kern = functools.partial(_proj_b_kernel, n_heads=n_heads, kv_rank=kv_rank, q_rank=q_rank)
    return pl.pallas_call(
        kern,
        grid=(t // tm,),
        in_specs=[
            pl.BlockSpec((tm, d), lambda i: (i, 0)),
            pl.BlockSpec((tm, 1), lambda i: (i, 0)),
            full(freq), full(g_kv), full(g_mix), full(w_down), full(g_lat), full(w_upk), full(w_upv),
            full(w_in), full(g_qa), full(w_qn), full(w_qp),
        ],
        out_specs=[
            pl.BlockSpec((tm, n_heads * QK_PAD), lambda i: (i, 0)),
            pl.BlockSpec((tm, n_heads * QK_PAD), lambda i: (i, 0)),
            pl.BlockSpec((tm, 2 * n_heads * HEAD_DIM), lambda i: (i, 0)),
            pl.BlockSpec((tm, mw), lambda i: (i, 0)),
        ],
        out_shape=[
            jax.ShapeDtypeStruct((t, n_heads * QK_PAD), BF16),
            jax.ShapeDtypeStruct((t, n_heads * QK_PAD), BF16),
            jax.ShapeDtypeStruct((t, 2 * n_heads * HEAD_DIM), BF16),
            jax.ShapeDtypeStruct((t, mw), F32),
        ],
        compiler_params=_params(("arbitrary",)),
        name="proj_b",
    )(h, pos, freq, g_kv, g_mix, w_down, g_lat, w_upk, w_upv, w_in, g_qa, w_qn, w_qp)


def _mla_kernel(q_ref, k_ref, v_ref, o_ref, sa_ref, sb_ref, m_ref, acc_ref, *, tq):
    qi = pl.program_id(2)
    m_ref[...] = jnp.full_like(m_ref, -1e30)
    acc_ref[...] = jnp.zeros_like(acc_ref)

    def scores(kb, s_ref):
        r0 = pl.multiple_of(kb * tq, tq)
        s_ref[...] = _dot_nt(q_ref[...], k_ref[pl.ds(r0, tq), :])

    def consume(kb, s_ref, diagonal):
        r0 = pl.multiple_of(kb * tq, tq)
        s = s_ref[...]
        if diagonal:
            qc = lax.broadcasted_iota(jnp.int32, (tq, tq), 0) // CHUNK
            kc = lax.broadcasted_iota(jnp.int32, (tq, tq), 1) // CHUNK
            s = jnp.where(kc <= qc, s, -1e30)
        m_old = m_ref[...]
        m_new = jnp.maximum(m_old, jnp.max(s, axis=-1, keepdims=True))
        alpha = jnp.exp2(m_old - m_new)
        p = jnp.exp2(s - pltpu.repeat(m_new, tq // 128, axis=1))
        pv = _dot(p.astype(BF16), v_ref[pl.ds(r0, tq), :])
        acc_ref[...] = pltpu.repeat(alpha, 2, axis=1) * acc_ref[...] + pv
        m_ref[...] = m_new

    scores(0, sa_ref)

    def pair(jj, c):
        scores(2 * jj + 1, sb_ref)
        consume(2 * jj, sa_ref, False)
        scores(2 * jj + 2, sa_ref)
        consume(2 * jj + 1, sb_ref, False)
        return c

    lax.fori_loop(0, qi // 2, pair, 0)

    @pl.when(qi % 2 == 1)
    def _():
        scores(qi, sb_ref)
        consume(qi - 1, sa_ref, False)
        consume(qi, sb_ref, True)

    @pl.when(qi % 2 == 0)
    def _():
        consume(qi, sa_ref, True)

    o_ref[...] = (acc_ref[:, :HEAD_DIM] / acc_ref[:, HEAD_DIM:]).astype(BF16)


def _mla(q, k, v, nb, seq, n_heads, tq):
    t = q.shape[0]
    nq = seq // tq
    kern = functools.partial(_mla_kernel, tq=tq)
    return pl.pallas_call(
        kern,
        grid=(nb, n_heads, nq),
        in_specs=[
            pl.BlockSpec((tq, QK_PAD), lambda b, h, i: (b * nq + i, h)),
            pl.BlockSpec((seq, QK_PAD), lambda b, h, i: (b, h)),
            pl.BlockSpec((seq, 2 * HEAD_DIM), lambda b, h, i: (b, h)),
        ],
        out_specs=pl.BlockSpec((tq, HEAD_DIM), lambda b, h, i: (b * nq + i, h)),
        out_shape=jax.ShapeDtypeStruct((t, n_heads * HEAD_DIM), BF16),
        scratch_shapes=[
            pltpu.VMEM((tq, tq), F32),
            pltpu.VMEM((tq, tq), F32),
            pltpu.VMEM((tq, 128), F32),
            pltpu.VMEM((tq, 2 * HEAD_DIM), F32),
        ],
        compiler_params=_params(("arbitrary", "arbitrary", "arbitrary")),
        name="mla",
    )(q, k, v)


def kernel(x, mem, positions, g_mix_pre, g_mix_post, g_mlp_pre, g_mlp_post, g_mem, w_mem_k, w_mem_v, w_o, w_ff1, w_ff2, a_w_in, a_conv_w, a_conv_b, a_w_rgate, a_b_rgate, a_w_igate, a_b_igate, a_lambda, b_w_in, b_g_qa, b_w_qb, kv_g_in, kv_w_down, kv_g_latent, kv_w_up):
    nb, seq, d = x.shape
    t = nb * seq
    depth = g_mix_pre.shape[0]
    n_a = a_w_in.shape[0]
    lru_width = a_conv_w.shape[2]
    n_heads = lru_width // HEAD_DIM
    kv_rank = kv_g_latent.shape[0]
    q_rank = b_g_qa.shape[1]
    assert kv_w_down.shape[1] == kv_rank + ROPE_DIM
    assert seq % 512 == 0 and t % 1024 == 0 and d % 128 == 0
    assert depth - n_a == 1, "the shared K/V projection is fused into the single MLA layer's projections"

    bf = lambda w: w.astype(BF16)
    mk, mv = _mem_kv(mem, g_mem, bf(w_mem_k), bf(w_mem_v))

    half = ROPE_DIM // 2
    inv_freq = ROPE_THETA ** (-jnp.arange(half, dtype=F32) / half)
    freq = jnp.tile(inv_freq, 128 // half).reshape(1, 128)
    pos = positions.reshape(t, 1)

    w_up = bf(kv_w_up).reshape(kv_rank, n_heads, 2, HEAD_DIM)
    w_upk = w_up[:, :, 0, :].reshape(kv_rank, n_heads * HEAD_DIM)
    w_upv = w_up[:, :, 1, :].reshape(kv_rank, n_heads * HEAD_DIM)

    h = x.reshape(t, d)
    for layer in range(depth):
        if layer < n_a:
            la = layer
            proj = _norm_matmul(h, g_mix_pre[layer], bf(a_w_in[la]), tm=512, tn=1792)
            tok = _rglru(proj, nb, seq, a_conv_w[la], a_conv_b[la], bf(a_w_rgate[la]), a_b_rgate[la],
                         bf(a_w_igate[la]), a_b_igate[la], a_lambda[la], ts=256)
            qm_src, qm_col = proj, (2 * lru_width) // (MEM_HEADS * HEAD_DIM)
        else:
            lb = layer - n_a
            w_qb = bf(b_w_qb[lb]).reshape(q_rank, n_heads, QK_HEAD_DIM)
            w_qn = w_qb[:, :, :HEAD_DIM].reshape(q_rank, n_heads * HEAD_DIM)
            w_qp = w_qb[:, :, HEAD_DIM:].reshape(q_rank, n_heads * ROPE_DIM)
            q, k, v, qm_src = _proj_b(h, pos, freq, kv_g_in, g_mix_pre[layer], bf(kv_w_down), kv_g_latent,
                                      w_upk, w_upv, bf(b_w_in[lb]), b_g_qa[lb], w_qn, w_qp, tm=512)
            qm_col = 0
            tok = _mla(q, k, v, nb, seq, n_heads, tq=512)
        h = _mix_out(tok, qm_src, qm_col, mk, mv, layer, bf(w_o[layer]), h, g_mix_post[layer], seq, tm=512)
        h = _mlp(h, g_mlp_pre[layer], bf(w_ff1[layer]), bf(w_ff2[layer]), g_mlp_post[layer], tm=1024, tf=512)
    return h.reshape(nb, seq, d)
```

```python
import functools

import jax
import jax.numpy as jnp
from jax import lax
from jax.experimental import pallas as pl
from jax.experimental.pallas import tpu as pltpu

EPS = 1e-6
LRU_C = 8.0
CHUNK = 64
ROPE_THETA = 10000.0
LOG2_E = 1.4426950408889634
MEM_HEADS = 4
HEAD_DIM = 128
ROPE_DIM = 64
QK_HEAD_DIM = HEAD_DIM + ROPE_DIM
QK_PAD = 256
SUBLANES = 8
VMEM_LIMIT = 56 * 1024 * 1024

BF16 = jnp.bfloat16
F32 = jnp.float32


def _params(sem, vmem=VMEM_LIMIT):
    return pltpu.CompilerParams(dimension_semantics=sem, vmem_limit_bytes=vmem)


def _const_spec(a):
    return pl.BlockSpec(a.shape, lambda *_: (0,) * a.ndim, pipeline_mode=pl.Buffered(1))


def _rms(x, g):
    return x * lax.rsqrt(jnp.mean(x * x, axis=-1, keepdims=True) + EPS) * g


def _dot(a, b):
    return jnp.dot(a, b, preferred_element_type=F32)


def _dot_nt(a, b):
    return lax.dot_general(a, b, (((1,), (1,)), ((), ())), preferred_element_type=F32)


def _mem_kv_kernel(mem_ref, g_ref, wk_ref, wv_ref, mk_ref, mv_ref):
    mn = _rms(mem_ref[0], g_ref[0]).astype(BF16)
    mk_ref[0, 0] = _dot(mn, wk_ref[0]).astype(BF16)
    mv = _dot(mn, wv_ref[0]).astype(BF16)
    ones = jnp.ones((mv.shape[0], HEAD_DIM), BF16)
    for hh in range(MEM_HEADS):
        mv_ref[0, 0, :, 2 * hh * HEAD_DIM:(2 * hh + 1) * HEAD_DIM] = mv[:, hh * HEAD_DIM:(hh + 1) * HEAD_DIM]
        mv_ref[0, 0, :, (2 * hh + 1) * HEAD_DIM:(2 * hh + 2) * HEAD_DIM] = ones


def _mem_kv(mem, g_mem, w_k, w_v):
    nb, ml, d = mem.shape
    nl, _, mw = w_k.shape
    assert mw == MEM_HEADS * HEAD_DIM
    return pl.pallas_call(
        _mem_kv_kernel,
        grid=(nl, nb),
        in_specs=[
            pl.BlockSpec((1, ml, d), lambda l, b: (b, 0, 0)),
            pl.BlockSpec((1, 1, d), lambda l, b: (l, 0, 0)),
            pl.BlockSpec((1, d, mw), lambda l, b: (l, 0, 0)),
            pl.BlockSpec((1, d, mw), lambda l, b: (l, 0, 0)),
        ],
        out_specs=[
            pl.BlockSpec((1, 1, ml, mw), lambda l, b: (l, b, 0, 0)),
            pl.BlockSpec((1, 1, ml, 2 * mw), lambda l, b: (l, b, 0, 0)),
        ],
        out_shape=[jax.ShapeDtypeStruct((nl, nb, ml, mw), BF16),
                   jax.ShapeDtypeStruct((nl, nb, ml, 2 * mw), BF16)],
        compiler_params=_params(("arbitrary", "arbitrary")),
        name="mem_kv",
    )(mem, g_mem.reshape(nl, 1, d), w_k, w_v)


def _lru_layer_kernel(x_ref, g_ref, w_ref, cw_ref, cb_ref, wg_ref, br_ref, bi_ref, lam_ref,
                      tok_ref, qm_ref, hn_ref, xpad_ref, a_ref, b_ref, gg_ref, carry_ref, *, ts, conv_width):
    c = cw_ref.shape[1]
    pad = SUBLANES
    pw = 2 * HEAD_DIM

    @pl.when(pl.program_id(1) == 0)
    def _():
        xpad_ref[0:pad, :] = jnp.zeros((pad, c), F32)
        carry_ref[...] = jnp.zeros_like(carry_ref)

    @pl.when(pl.program_id(1) != 0)
    def _():
        xpad_ref[0:pad, :] = xpad_ref[ts:ts + pad, :]

    hn_ref[...] = _rms(x_ref[...], g_ref[...]).astype(BF16)
    sp = jax.nn.softplus(-lam_ref[...])

    def x_branch(p):
        sl = slice(p * pw, (p + 1) * pw)
        xpad_ref[pad:pad + ts, sl] = _dot(hn_ref[...], w_ref[:, sl])

    x_branch(0)
    for p in range(c // pw):
        sl = slice(p * pw, (p + 1) * pw)
        if p + 1 < c // pw:
            x_branch(p + 1)
        xc = cb_ref[:, sl] + jnp.zeros((ts, pw), F32)
        for k in range(conv_width):
            off = pad - (conv_width - 1) + k
            xc = xc + cw_ref[k:k + 1, sl] * xpad_ref[off:off + ts, sl]
        ri = _dot(xc.astype(BF16), wg_ref[p])
        r = jax.nn.sigmoid(ri[:, :pw] + br_ref[:, sl])
        i = jax.nn.sigmoid(ri[:, pw:] + bi_ref[:, sl])
        log_a = (-LRU_C * r) * sp[:, sl]
        a = jnp.exp(log_a)
        a_ref[:, sl] = a
        b_ref[:, sl] = jnp.sqrt(jnp.tanh(-log_a) * (1.0 + a * a)) * (i * xc)
        gg_ref[:, sl] = jax.nn.gelu(_dot(hn_ref[...], w_ref[:, c + p * pw:c + (p + 1) * pw]))
    qm_ref[...] = _dot(hn_ref[...], w_ref[:, 2 * c:])

    row = lax.broadcasted_iota(jnp.int32, (SUBLANES, c), 0)

    def body(g, carry):
        r0 = pl.multiple_of(g * SUBLANES, SUBLANES)
        a = a_ref[pl.ds(r0, SUBLANES), :]
        b = b_ref[pl.ds(r0, SUBLANES), :]
        for d in (1, 2, 4):
            keep = row >= d
            a_sh = pltpu.roll(a, d, 0)
            b_sh = pltpu.roll(b, d, 0)
            b = jnp.where(keep, a * b_sh + b, b)
            a = jnp.where(keep, a * a_sh, a)
        h = b + a * carry
        b_ref[pl.ds(r0, SUBLANES), :] = h
        return jnp.broadcast_to(h[SUBLANES - 1:SUBLANES, :], (SUBLANES, c))

    carry_ref[...] = lax.fori_loop(0, ts // SUBLANES, body, carry_ref[...])

    tok_ref[...] = (b_ref[...] * gg_ref[...]).astype(BF16)


def _lru_layer(x, g, w_in, nb, seq, cw, cb, wg, br, bi, lam, ts):
    t, d = x.shape
    c = cw.shape[1]
    mw = w_in.shape[1] - 2 * c
    spb = seq // ts
    g, cb, br, bi, lam = (v.reshape(1, -1) for v in (g, cb, br, bi, lam))
    kern = functools.partial(_lru_layer_kernel, ts=ts, conv_width=cw.shape[0])
    return pl.pallas_call(
        kern,
        grid=(nb, spb),
        in_specs=[pl.BlockSpec((ts, d), lambda b, s: (b * spb + s, 0))]
                 + [_const_spec(v) for v in (g, w_in, cw, cb, wg, br, bi, lam)],
        out_specs=[
            pl.BlockSpec((ts, c), lambda b, s: (b * spb + s, 0)),
            pl.BlockSpec((ts, mw), lambda b, s: (b * spb + s, 0)),
        ],
        out_shape=[jax.ShapeDtypeStruct((t, c), BF16), jax.ShapeDtypeStruct((t, mw), F32)],
        scratch_shapes=[
            pltpu.VMEM((ts, d), BF16),
            pltpu.VMEM((ts + SUBLANES, c), F32),
            pltpu.VMEM((ts, c), F32),
            pltpu.VMEM((ts, c), F32),
            pltpu.VMEM((ts, c), F32),
            pltpu.VMEM((SUBLANES, c), F32),
        ],
        compiler_params=_params(("arbitrary", "arbitrary")),
        name="lru_layer",
    )(x, g, w_in, cw, cb, wg, br, bi, lam)


def _pair_gate_weights(w_r, w_i):
    n, bd, _ = w_r.shape
    z = jnp.zeros((n // 2, bd, bd), w_r.dtype)

    def blockdiag(w):
        w = w.reshape(n // 2, 2, bd, bd)
        top = jnp.concatenate([w[:, 0], z], axis=2)
        bot = jnp.concatenate([z, w[:, 1]], axis=2)
        return jnp.concatenate([top, bot], axis=1)

    return jnp.concatenate([blockdiag(w_r), blockdiag(w_i)], axis=2)


def _mix_out_kernel(tok_ref, qm_ref, mk_ref, mv_ref, wo_ref, h_ref, g_ref, o_ref, *, tok_width):
    scale = LOG2_E * HEAD_DIM ** -0.5
    y = _dot(tok_ref[...], wo_ref[0:tok_width, :])
    heads = []
    for hh in range(MEM_HEADS):
        sl = slice(hh * HEAD_DIM, (hh + 1) * HEAD_DIM)
        q = (qm_ref[:, sl] * scale).astype(BF16)
        s = _dot_nt(q, mk_ref[0, 0, :, sl])
        p = jnp.exp2(s - jnp.max(s, axis=-1, keepdims=True))
        pv = _dot(p.astype(BF16), mv_ref[0, 0, :, 2 * hh * HEAD_DIM:(2 * hh + 2) * HEAD_DIM])
        heads.append((pv[:, :HEAD_DIM] / pv[:, HEAD_DIM:]).astype(BF16))
    y = y + _dot(jnp.concatenate(heads, axis=1), wo_ref[tok_width:, :])
    o_ref[...] = h_ref[...] + _rms(y, g_ref[...])


def _mix_out(tok, qm, mk, mv, layer, w_o, h, g_post, seq, tm):
    t, tok_width = tok.shape
    d = h.shape[1]
    ml, mw = mk.shape[2], mk.shape[3]
    bpb = seq // tm
    kern = functools.partial(_mix_out_kernel, tok_width=tok_width)
    return pl.pallas_call(
        kern,
        grid=(t // tm,),
        in_specs=[
            pl.BlockSpec((tm, tok_width), lambda i: (i, 0)),
            pl.BlockSpec((tm, mw), lambda i: (i, 0)),
            pl.BlockSpec((1, 1, ml, mw), lambda i: (layer, i // bpb, 0, 0)),
            pl.BlockSpec((1, 1, ml, 2 * mw), lambda i: (layer, i // bpb, 0, 0)),
            pl.BlockSpec((None, tok_width + mw, d), lambda i: (layer, 0, 0), pipeline_mode=pl.Buffered(1)),
            pl.BlockSpec((tm, d), lambda i: (i, 0)),
            pl.BlockSpec((1, d), lambda i: (0, 0)),
        ],
        out_specs=pl.BlockSpec((tm, d), lambda i: (i, 0)),
        out_shape=jax.ShapeDtypeStruct((t, d), F32),
        compiler_params=_params(("arbitrary",)),
        name="mix_out",
    )(tok, qm, mk, mv, w_o, h, g_post.reshape(1, d))


def _mlp_kernel(h_ref, gpre_ref, w1_ref, w2_ref, gpost_ref, o_ref, hn_ref):
    j = pl.program_id(1)

    @pl.when(j == 0)
    def _():
        hn_ref[...] = _rms(h_ref[...], gpre_ref[...]).astype(BF16)
        o_ref[...] = jnp.zeros_like(o_ref)

    f = _dot(hn_ref[...], w1_ref[...])
    f = jnp.square(jnp.maximum(f, 0.0)).astype(BF16)
    o_ref[...] += _dot(f, w2_ref[...])

    @pl.when(j == pl.num_programs(1) - 1)
    def _():
        o_ref[...] = h_ref[...] + _rms(o_ref[...], gpost_ref[...])


def _mlp(h, g_pre, w1, w2, layer, g_post, tm, tf):
    t, d = h.shape
    dff = w1.shape[2]
    return pl.pallas_call(
        _mlp_kernel,
        grid=(t // tm, dff // tf),
        in_specs=[
            pl.BlockSpec((tm, d), lambda i, j: (i, 0)),
            pl.BlockSpec((1, d), lambda i, j: (0, 0)),
            pl.BlockSpec((None, d, tf), lambda i, j: (layer, 0, j)),
            pl.BlockSpec((None, tf, d), lambda i, j: (layer, j, 0)),
            pl.BlockSpec((1, d), lambda i, j: (0, 0)),
        ],
        out_specs=pl.BlockSpec((tm, d), lambda i, j: (i, 0)),
        out_shape=jax.ShapeDtypeStruct((t, d), F32),
        scratch_shapes=[pltpu.VMEM((tm, d), BF16)],
        compiler_params=_params(("arbitrary", "arbitrary")),
        name="mlp",
    )(h, g_pre.reshape(1, d), w1, w2, g_post.reshape(1, d))


def _swap_halves(x, lane):
    return jnp.where(lane % ROPE_DIM < ROPE_DIM // 2, pltpu.roll(x, 128 - ROPE_DIM // 2, 1),
                     pltpu.roll(x, ROPE_DIM // 2, 1))


def _proj_b_kernel(h_ref, pos_ref, freq_ref, gkv_ref, gmix_ref, wdown_ref, glat_ref, wupk_ref, wupv_ref,
                   win_ref, gqa_ref, wqn_ref, wqp_ref,
                   q_ref, k_ref, v_ref, qm_ref, *, n_heads, kv_rank, q_rank):
    h = h_ref[...]
    tm = h.shape[0]
    xn = h * lax.rsqrt(jnp.mean(h * h, axis=-1, keepdims=True) + EPS)

    lane = lax.broadcasted_iota(jnp.int32, (tm, 128), 1)
    ang = pos_ref[...].astype(F32) * freq_ref[...]
    cos = jnp.cos(ang)
    sin = jnp.where(lane % ROPE_DIM < ROPE_DIM // 2, -jnp.sin(ang), jnp.sin(ang))
    low = lane < ROPE_DIM

    def rope(x):
        return x * cos + _swap_halves(x, lane) * sin

    ckv = _dot((xn * gkv_ref[...]).astype(BF16), wdown_ref[...])
    latent = _rms(ckv[:, :kv_rank], glat_ref[...]).astype(BF16)
    kpe_raw = jnp.concatenate([ckv[:, kv_rank:], jnp.zeros((tm, 128 - ROPE_DIM), F32)], axis=1)
    kpe = jnp.where(low, rope(kpe_raw), 0.0).astype(BF16)
    k_nope = _dot(latent, wupk_ref[...])
    v = _dot(latent, wupv_ref[...])
    ones = jnp.ones((tm, HEAD_DIM), BF16)
    for hh in range(n_heads):
        sl = slice(hh * HEAD_DIM, (hh + 1) * HEAD_DIM)
        k_ref[:, hh * QK_PAD:hh * QK_PAD + HEAD_DIM] = k_nope[:, sl].astype(BF16)
        k_ref[:, hh * QK_PAD + HEAD_DIM:(hh + 1) * QK_PAD] = kpe
        v_ref[:, 2 * hh * HEAD_DIM:(2 * hh + 1) * HEAD_DIM] = v[:, sl].astype(BF16)
        v_ref[:, (2 * hh + 1) * HEAD_DIM:(2 * hh + 2) * HEAD_DIM] = ones

    proj = _dot((xn * gmix_ref[...]).astype(BF16), win_ref[...])
    qm_ref[...] = proj[:, q_rank:]
    cq = _rms(proj[:, :q_rank], gqa_ref[...]).astype(BF16)
    scale = LOG2_E * QK_HEAD_DIM ** -0.5
    q_nope = _dot(cq, wqn_ref[...]) * scale
    q_pe = _dot(cq, wqp_ref[...]) * scale
    for hp in range(n_heads // 2):
        pe = rope(q_pe[:, hp * 128:(hp + 1) * 128])
        for e in range(2):
            hh = 2 * hp + e
            q_ref[:, hh * QK_PAD:hh * QK_PAD + HEAD_DIM] = q_nope[:, hh * HEAD_DIM:(hh + 1) * HEAD_DIM].astype(BF16)
            pe_h = pe if e == 0 else pltpu.roll(pe, ROPE_DIM, 1)
            q_ref[:, hh * QK_PAD + HEAD_DIM:(hh + 1) * QK_PAD] = jnp.where(low, pe_h, 0.0).astype(BF16)


def _proj_b(h, pos, freq, g_kv, g_mix, w_down, g_lat, w_upk, w_upv, w_in, g_qa, w_qn, w_qp, tm):
    t, d = h.shape
    kv_rank = w_upk.shape[0]
    q_rank = w_qn.shape[0]
    n_heads = w_upk.shape[1] // HEAD_DIM
    mw = w_in.shape[1] - q_rank
    g_kv, g_mix, g_lat, g_qa = (g_kv.reshape(1, -1), g_mix.reshape(1, -1), g_lat.reshape(1, -1),
                                g_qa.reshape(1, -1))
    consts = (freq, g_kv, g_mix, w_down, g_lat, w_upk, w_upv, w_in, g_qa, w_qn, w_qp)
    kern = functools.partial(_proj_b_kernel, n_heads=n_heads, kv_rank=kv_rank, q_rank=q_rank)
    return pl.pallas_call(
        kern,
        grid=(t // tm,),
        in_specs=[
            pl.BlockSpec((tm, d), lambda i: (i, 0)),
            pl.BlockSpec((tm, 1), lambda i: (i, 0)),
        ] + [_const_spec(a) for a in consts],
        out_specs=[
            pl.BlockSpec((tm, n_heads * QK_PAD), lambda i: (i, 0)),
            pl.BlockSpec((tm, n_heads * QK_PAD), lambda i: (i, 0)),
            pl.BlockSpec((tm, 2 * n_heads * HEAD_DIM), lambda i: (i, 0)),
            pl.BlockSpec((tm, mw), lambda i: (i, 0)),
        ],
        out_shape=[
            jax.ShapeDtypeStruct((t, n_heads * QK_PAD), BF16),
            jax.ShapeDtypeStruct((t, n_heads * QK_PAD), BF16),
            jax.ShapeDtypeStruct((t, 2 * n_heads * HEAD_DIM), BF16),
            jax.ShapeDtypeStruct((t, mw), F32),
        ],
        compiler_params=_params(("arbitrary",)),
        name="proj_b",
    )(h, pos, *consts)


def _mla_kernel(q_ref, k_ref, v_ref, o_ref, sa_ref, sb_ref, m_ref, acc_ref, *, tq):
    qi = pl.program_id(2)
    m_ref[...] = jnp.full_like(m_ref, -1e30)
    acc_ref[...] = jnp.zeros_like(acc_ref)

    def scores(kb, s_ref):
        r0 = pl.multiple_of(kb * tq, tq)
        s_ref[...] = _dot_nt(q_ref[...], k_ref[pl.ds(r0, tq), :])

    def consume(kb, s_ref, diagonal):
        r0 = pl.multiple_of(kb * tq, tq)
        s = s_ref[...]
        if diagonal:
            qc = lax.broadcasted_iota(jnp.int32, (tq, tq), 0) // CHUNK
            kc = lax.broadcasted_iota(jnp.int32, (tq, tq), 1) // CHUNK
            s = jnp.where(kc <= qc, s, -1e30)
        m_old = m_ref[...]
        m_new = jnp.maximum(m_old, jnp.max(s, axis=-1, keepdims=True))
        alpha = jnp.exp2(m_old - m_new)
        p = jnp.exp2(s - jnp.concatenate([m_new] * (tq // 128), axis=1))
        pv = _dot(p.astype(BF16), v_ref[pl.ds(r0, tq), :])
        acc_ref[...] = jnp.concatenate([alpha, alpha], axis=1) * acc_ref[...] + pv
        m_ref[...] = m_new

    scores(0, sa_ref)

    def pair(jj, c):
        scores(2 * jj + 1, sb_ref)
        consume(2 * jj, sa_ref, False)
        scores(2 * jj + 2, sa_ref)
        consume(2 * jj + 1, sb_ref, False)
        return c

    lax.fori_loop(0, qi // 2, pair, 0)

    @pl.when(qi % 2 == 1)
    def _():
        scores(qi, sb_ref)
        consume(qi - 1, sa_ref, False)
        consume(qi, sb_ref, True)

    @pl.when(qi % 2 == 0)
    def _():
        consume(qi, sa_ref, True)

    o_ref[...] = (acc_ref[:, :HEAD_DIM] / acc_ref[:, HEAD_DIM:]).astype(BF16)


def _mla(q, k, v, nb, seq, n_heads, tq):
    t = q.shape[0]
    nq = seq // tq
    kern = functools.partial(_mla_kernel, tq=tq)
    return pl.pallas_call(
        kern,
        grid=(nb, n_heads, nq),
        in_specs=[
            pl.BlockSpec((tq, QK_PAD), lambda b, h, i: (b * nq + i, h)),
            pl.BlockSpec((seq, QK_PAD), lambda b, h, i: (b, h)),
            pl.BlockSpec((seq, 2 * HEAD_DIM), lambda b, h, i: (b, h)),
        ],
        out_specs=pl.BlockSpec((tq, HEAD_DIM), lambda b, h, i: (b * nq + i, h)),
        out_shape=jax.ShapeDtypeStruct((t, n_heads * HEAD_DIM), BF16),
        scratch_shapes=[
            pltpu.VMEM((tq, tq), F32),
            pltpu.VMEM((tq, tq), F32),
            pltpu.VMEM((tq, 128), F32),
            pltpu.VMEM((tq, 2 * HEAD_DIM), F32),
        ],
        compiler_params=_params(("arbitrary", "arbitrary", "arbitrary")),
        name="mla",
    )(q, k, v)


def kernel(x, mem, positions, g_mix_pre, g_mix_post, g_mlp_pre, g_mlp_post, g_mem, w_mem_k, w_mem_v, w_o, w_ff1, w_ff2, a_w_in, a_conv_w, a_conv_b, a_w_rgate, a_b_rgate, a_w_igate, a_b_igate, a_lambda, b_w_in, b_g_qa, b_w_qb, kv_g_in, kv_w_down, kv_g_latent, kv_w_up):
    nb, seq, d = x.shape
    t = nb * seq
    depth = g_mix_pre.shape[0]
    n_a = a_w_in.shape[0]
    lru_width = a_conv_w.shape[2]
    n_heads = lru_width // HEAD_DIM
    kv_rank = kv_g_latent.shape[0]
    q_rank = b_g_qa.shape[1]
    assert kv_w_down.shape[1] == kv_rank + ROPE_DIM
    assert seq % 512 == 0 and t % 1024 == 0 and d % 128 == 0
    assert depth - n_a == 1, "the shared K/V projection is fused into the single MLA layer's projections"

    bf = lambda w: w.astype(BF16)
    mk, mv = _mem_kv(mem, g_mem, bf(w_mem_k), bf(w_mem_v))
    w_o16, w_ff1_16, w_ff2_16 = bf(w_o), bf(w_ff1), bf(w_ff2)

    half = ROPE_DIM // 2
    inv_freq = ROPE_THETA ** (-jnp.arange(half, dtype=F32) / half)
    freq = jnp.tile(inv_freq, 128 // half).reshape(1, 128)
    pos = positions.reshape(t, 1)

    w_up = bf(kv_w_up).reshape(kv_rank, n_heads, 2, HEAD_DIM)
    w_upk = w_up[:, :, 0, :].reshape(kv_rank, n_heads * HEAD_DIM)
    w_upv = w_up[:, :, 1, :].reshape(kv_rank, n_heads * HEAD_DIM)

    h = x.reshape(t, d)
    for layer in range(depth):
        if layer < n_a:
            la = layer
            wg = _pair_gate_weights(bf(a_w_rgate[la]), bf(a_w_igate[la]))
            tok, qm = _lru_layer(h, g_mix_pre[layer], bf(a_w_in[la]), nb, seq, a_conv_w[la], a_conv_b[la], wg,
                                 a_b_rgate[la], a_b_igate[la], a_lambda[la], ts=512)
        else:
            lb = layer - n_a
            w_qb = bf(b_w_qb[lb]).reshape(q_rank, n_heads, QK_HEAD_DIM)
            w_qn = w_qb[:, :, :HEAD_DIM].reshape(q_rank, n_heads * HEAD_DIM)
            w_qp = w_qb[:, :, HEAD_DIM:].reshape(q_rank, n_heads * ROPE_DIM)
            q, k, v, qm = _proj_b(h, pos, freq, kv_g_in, g_mix_pre[layer], bf(kv_w_down), kv_g_latent,
                                  w_upk, w_upv, bf(b_w_in[lb]), b_g_qa[lb], w_qn, w_qp, tm=512)
            tok = _mla(q, k, v, nb, seq, n_heads, tq=512)
        h = _mix_out(tok, qm, mk, mv, layer, w_o16, h, g_mix_post[layer], seq, tm=512)
        h = _mlp(h, g_mlp_pre[layer], w_ff1_16, w_ff2_16, layer, g_mlp_post[layer], tm=1024, tf=512)
    return h.reshape(nb, seq, d)
```

```python
import functools

import jax
import jax.numpy as jnp
from jax import lax
from jax.experimental import pallas as pl
from jax.experimental.pallas import tpu as pltpu

EPS = 1e-6
LRU_C = 8.0
CHUNK = 64
ROPE_THETA = 10000.0
LOG2_E = 1.4426950408889634
MEM_HEADS = 4
HEAD_DIM = 128
ROPE_DIM = 64
QK_HEAD_DIM = HEAD_DIM + ROPE_DIM
QK_PAD = 256
SUBLANES = 8
VMEM_LIMIT = 56 * 1024 * 1024

BF16 = jnp.bfloat16
F32 = jnp.float32


def _params(sem, vmem=VMEM_LIMIT):
    return pltpu.CompilerParams(dimension_semantics=sem, vmem_limit_bytes=vmem)


def _const_spec(a):
    return pl.BlockSpec(a.shape, lambda *_: (0,) * a.ndim, pipeline_mode=pl.Buffered(1))


def _rms(x, g):
    return x * lax.rsqrt(jnp.mean(x * x, axis=-1, keepdims=True) + EPS) * g


def _dot(a, b):
    return jnp.dot(a, b, preferred_element_type=F32)


def _dot_nt(a, b):
    return lax.dot_general(a, b, (((1,), (1,)), ((), ())), preferred_element_type=F32)


def _mem_kv_kernel(mem_ref, g_ref, wk_ref, wv_ref, mk_ref, mv_ref):
    mn = _rms(mem_ref[0], g_ref[0]).astype(BF16)
    mk_ref[0, 0] = _dot(mn, wk_ref[0]).astype(BF16)
    mv = _dot(mn, wv_ref[0]).astype(BF16)
    ones = jnp.ones((mv.shape[0], HEAD_DIM), BF16)
    for hh in range(MEM_HEADS):
        mv_ref[0, 0, :, 2 * hh * HEAD_DIM:(2 * hh + 1) * HEAD_DIM] = mv[:, hh * HEAD_DIM:(hh + 1) * HEAD_DIM]
        mv_ref[0, 0, :, (2 * hh + 1) * HEAD_DIM:(2 * hh + 2) * HEAD_DIM] = ones


def _mem_kv(mem, g_mem, w_k, w_v):
    nb, ml, d = mem.shape
    nl, _, mw = w_k.shape
    assert mw == MEM_HEADS * HEAD_DIM
    return pl.pallas_call(
        _mem_kv_kernel,
        grid=(nl, nb),
        in_specs=[
            pl.BlockSpec((1, ml, d), lambda l, b: (b, 0, 0)),
            pl.BlockSpec((1, 1, d), lambda l, b: (l, 0, 0)),
            pl.BlockSpec((1, d, mw), lambda l, b: (l, 0, 0)),
            pl.BlockSpec((1, d, mw), lambda l, b: (l, 0, 0)),
        ],
        out_specs=[
            pl.BlockSpec((1, 1, ml, mw), lambda l, b: (l, b, 0, 0)),
            pl.BlockSpec((1, 1, ml, 2 * mw), lambda l, b: (l, b, 0, 0)),
        ],
        out_shape=[jax.ShapeDtypeStruct((nl, nb, ml, mw), BF16),
                   jax.ShapeDtypeStruct((nl, nb, ml, 2 * mw), BF16)],
        compiler_params=_params(("arbitrary", "arbitrary")),
        name="mem_kv",
    )(mem, g_mem.reshape(nl, 1, d), w_k, w_v)


def _lru_layer_kernel(x_ref, g_ref, w_ref, cw_ref, cb_ref, wg_ref, br_ref, bi_ref, lam_ref,
                      tok_ref, qm_ref, hn_ref, xpad_ref, a_ref, b_ref, gg_ref, carry_ref, *, ts, conv_width):
    c = cw_ref.shape[1]
    pad = SUBLANES
    pw = 2 * HEAD_DIM

    @pl.when(pl.program_id(1) == 0)
    def _():
        xpad_ref[0:pad, :] = jnp.zeros((pad, c), F32)
        carry_ref[...] = jnp.zeros_like(carry_ref)

    @pl.when(pl.program_id(1) != 0)
    def _():
        xpad_ref[0:pad, :] = xpad_ref[ts:ts + pad, :]

    hn_ref[...] = _rms(x_ref[...], g_ref[...]).astype(BF16)
    sp = jax.nn.softplus(-lam_ref[...])

    def x_branch(p):
        sl = slice(p * pw, (p + 1) * pw)
        xpad_ref[pad:pad + ts, sl] = _dot(hn_ref[...], w_ref[:, sl])

    x_branch(0)
    for p in range(c // pw):
        sl = slice(p * pw, (p + 1) * pw)
        if p + 1 < c // pw:
            x_branch(p + 1)
        xc = cb_ref[:, sl] + jnp.zeros((ts, pw), F32)
        for k in range(conv_width):
            off = pad - (conv_width - 1) + k
            xc = xc + cw_ref[k:k + 1, sl] * xpad_ref[off:off + ts, sl]
        ri = _dot(xc.astype(BF16), wg_ref[p])
        r = jax.nn.sigmoid(ri[:, :pw] + br_ref[:, sl])
        i = jax.nn.sigmoid(ri[:, pw:] + bi_ref[:, sl])
        log_a = (-LRU_C * r) * sp[:, sl]
        a = jnp.exp(log_a)
        a_ref[:, sl] = a
        b_ref[:, sl] = jnp.sqrt(jnp.tanh(-log_a) * (1.0 + a * a)) * (i * xc)
        gg_ref[:, sl] = jax.nn.gelu(_dot(hn_ref[...], w_ref[:, c + p * pw:c + (p + 1) * pw]))
    qm_ref[...] = _dot(hn_ref[...], w_ref[:, 2 * c:])

    row = lax.broadcasted_iota(jnp.int32, (SUBLANES, c), 0)

    def body(g, carry):
        r0 = pl.multiple_of(g * SUBLANES, SUBLANES)
        a = a_ref[pl.ds(r0, SUBLANES), :]
        b = b_ref[pl.ds(r0, SUBLANES), :]
        for d in (1, 2, 4):
            keep = row >= d
            a_sh = pltpu.roll(a, d, 0)
            b_sh = pltpu.roll(b, d, 0)
            b = jnp.where(keep, a * b_sh + b, b)
            a = jnp.where(keep, a * a_sh, a)
        h = b + a * carry
        b_ref[pl.ds(r0, SUBLANES), :] = h
        return jnp.broadcast_to(h[SUBLANES - 1:SUBLANES, :], (SUBLANES, c))

    carry_ref[...] = lax.fori_loop(0, ts // SUBLANES, body, carry_ref[...])

    tok_ref[...] = (b_ref[...] * gg_ref[...]).astype(BF16)


def _lru_layer(x, g, w_in, nb, seq, cw, cb, wg, br, bi, lam, ts):
    t, d = x.shape
    c = cw.shape[1]
    mw = w_in.shape[1] - 2 * c
    spb = seq // ts
    g, cb, br, bi, lam = (v.reshape(1, -1) for v in (g, cb, br, bi, lam))
    kern = functools.partial(_lru_layer_kernel, ts=ts, conv_width=cw.shape[0])
    return pl.pallas_call(
        kern,
        grid=(nb, spb),
        in_specs=[pl.BlockSpec((ts, d), lambda b, s: (b * spb + s, 0))]
                 + [_const_spec(v) for v in (g, w_in, cw, cb, wg, br, bi, lam)],
        out_specs=[
            pl.BlockSpec((ts, c), lambda b, s: (b * spb + s, 0)),
            pl.BlockSpec((ts, mw), lambda b, s: (b * spb + s, 0)),
        ],
        out_shape=[jax.ShapeDtypeStruct((t, c), BF16), jax.ShapeDtypeStruct((t, mw), F32)],
        scratch_shapes=[
            pltpu.VMEM((ts, d), BF16),
            pltpu.VMEM((ts + SUBLANES, c), F32),
            pltpu.VMEM((ts, c), F32),
            pltpu.VMEM((ts, c), F32),
            pltpu.VMEM((ts, c), F32),
            pltpu.VMEM((SUBLANES, c), F32),
        ],
        compiler_params=_params(("arbitrary", "arbitrary")),
        name="lru_layer",
    )(x, g, w_in, cw, cb, wg, br, bi, lam)


def _pair_gate_weights(w_r, w_i):
    n, bd, _ = w_r.shape
    z = jnp.zeros((n // 2, bd, bd), w_r.dtype)

    def blockdiag(w):
        w = w.reshape(n // 2, 2, bd, bd)
        top = jnp.concatenate([w[:, 0], z], axis=2)
        bot = jnp.concatenate([z, w[:, 1]], axis=2)
        return jnp.concatenate([top, bot], axis=1)

    return jnp.concatenate([blockdiag(w_r), blockdiag(w_i)], axis=2)


def _mix_out_kernel(tok_ref, qm_ref, mk_ref, mv_ref, wo_ref, h_ref, g_ref, o_ref, *, tok_width, rc):
    scale = LOG2_E * HEAD_DIM ** -0.5
    for r in range(tok_ref.shape[0] // rc):
        rw = pl.ds(r * rc, rc)
        y = _dot(tok_ref[rw, :], wo_ref[0:tok_width, :])
        heads = []
        for hh in range(MEM_HEADS):
            sl = slice(hh * HEAD_DIM, (hh + 1) * HEAD_DIM)
            q = (qm_ref[rw, sl] * scale).astype(BF16)
            s = _dot_nt(q, mk_ref[0, 0, :, sl])
            p = jnp.exp2(s - jnp.max(s, axis=-1, keepdims=True))
            pv = _dot(p.astype(BF16), mv_ref[0, 0, :, 2 * hh * HEAD_DIM:(2 * hh + 2) * HEAD_DIM])
            heads.append((pv[:, :HEAD_DIM] / pv[:, HEAD_DIM:]).astype(BF16))
        y = y + _dot(jnp.concatenate(heads, axis=1), wo_ref[tok_width:, :])
        o_ref[rw, :] = h_ref[rw, :] + _rms(y, g_ref[...])


def _mix_out(tok, qm, mk, mv, layer, w_o, h, g_post, seq, tm):
    t, tok_width = tok.shape
    d = h.shape[1]
    ml, mw = mk.shape[2], mk.shape[3]
    bpb = seq // tm
    kern = functools.partial(_mix_out_kernel, tok_width=tok_width, rc=tm)
    return pl.pallas_call(
        kern,
        grid=(t // tm,),
        in_specs=[
            pl.BlockSpec((tm, tok_width), lambda i: (i, 0)),
            pl.BlockSpec((tm, mw), lambda i: (i, 0)),
            pl.BlockSpec((1, 1, ml, mw), lambda i: (layer, i // bpb, 0, 0)),
            pl.BlockSpec((1, 1, ml, 2 * mw), lambda i: (layer, i // bpb, 0, 0)),
            pl.BlockSpec((None, tok_width + mw, d), lambda i: (layer, 0, 0), pipeline_mode=pl.Buffered(1)),
            pl.BlockSpec((tm, d), lambda i: (i, 0)),
            pl.BlockSpec((1, d), lambda i: (0, 0)),
        ],
        out_specs=pl.BlockSpec((tm, d), lambda i: (i, 0)),
        out_shape=jax.ShapeDtypeStruct((t, d), F32),
        compiler_params=_params(("arbitrary",)),
        name="mix_out",
    )(tok, qm, mk, mv, w_o, h, g_post.reshape(1, d))


def _mlp_kernel(h_ref, gpre_ref, w1_ref, w2_ref, gpost_ref, o_ref, hn_ref, *, rc):
    j = pl.program_id(1)
    last = pl.num_programs(1) - 1
    tm = h_ref.shape[0]

    def ffn(hn):
        f = _dot(hn, w1_ref[...])
        return _dot(jnp.square(jnp.maximum(f, 0.0)).astype(BF16), w2_ref[...])

    @pl.when(j == 0)
    def _():
        for r in range(tm // rc):
            rows = pl.ds(r * rc, rc)
            hn = _rms(h_ref[rows, :], gpre_ref[...]).astype(BF16)
            hn_ref[rows, :] = hn
            o_ref[rows, :] = ffn(hn)

    @pl.when(jnp.logical_and(j > 0, j < last))
    def _():
        o_ref[...] += ffn(hn_ref[...])

    @pl.when(j == last)
    def _():
        for r in range(tm // rc):
            rows = pl.ds(r * rc, rc)
            acc = o_ref[rows, :] + ffn(hn_ref[rows, :])
            o_ref[rows, :] = h_ref[rows, :] + _rms(acc, gpost_ref[...])


def _mlp(h, g_pre, w1, w2, layer, g_post, tm, tf):
    t, d = h.shape
    dff = w1.shape[2]
    assert dff // tf >= 2
    return pl.pallas_call(
        functools.partial(_mlp_kernel, rc=512),
        grid=(t // tm, dff // tf),
        in_specs=[
            pl.BlockSpec((tm, d), lambda i, j: (i, 0)),
            pl.BlockSpec((1, d), lambda i, j: (0, 0)),
            pl.BlockSpec((None, d, tf), lambda i, j: (layer, 0, j)),
            pl.BlockSpec((None, tf, d), lambda i, j: (layer, j, 0)),
            pl.BlockSpec((1, d), lambda i, j: (0, 0)),
        ],
        out_specs=pl.BlockSpec((tm, d), lambda i, j: (i, 0)),
        out_shape=jax.ShapeDtypeStruct((t, d), F32),
        scratch_shapes=[pltpu.VMEM((tm, d), BF16)],
        compiler_params=_params(("arbitrary", "arbitrary")),
        name="mlp",
    )(h, g_pre.reshape(1, d), w1, w2, g_post.reshape(1, d))


def _swap_halves(x, lane):
    return jnp.where(lane % ROPE_DIM < ROPE_DIM // 2, pltpu.roll(x, 128 - ROPE_DIM // 2, 1),
                     pltpu.roll(x, ROPE_DIM // 2, 1))


def _proj_b_kernel(h_ref, pos_ref, freq_ref, gkv_ref, gmix_ref, wdown_ref, glat_ref, wupk_ref, wupv_ref,
                   win_ref, gqa_ref, wqn_ref, wqp_ref,
                   q_ref, k_ref, v_ref, qm_ref, *, n_heads, kv_rank, q_rank, rc):
    for r in range(h_ref.shape[0] // rc):
        _proj_b_rows(pl.ds(r * rc, rc), rc, h_ref, pos_ref, freq_ref, gkv_ref, gmix_ref, wdown_ref, glat_ref,
                     wupk_ref, wupv_ref, win_ref, gqa_ref, wqn_ref, wqp_ref, q_ref, k_ref, v_ref, qm_ref,
                     n_heads, kv_rank, q_rank)


def _proj_b_rows(rw, tm, h_ref, pos_ref, freq_ref, gkv_ref, gmix_ref, wdown_ref, glat_ref, wupk_ref, wupv_ref,
                 win_ref, gqa_ref, wqn_ref, wqp_ref, q_ref, k_ref, v_ref, qm_ref, n_heads, kv_rank, q_rank):
    h = h_ref[rw, :]
    xn = h * lax.rsqrt(jnp.mean(h * h, axis=-1, keepdims=True) + EPS)

    lane = lax.broadcasted_iota(jnp.int32, (tm, 128), 1)
    ang = pos_ref[rw, :].astype(F32) * freq_ref[...]
    cos = jnp.cos(ang)
    sin = jnp.where(lane % ROPE_DIM < ROPE_DIM // 2, -jnp.sin(ang), jnp.sin(ang))
    low = lane < ROPE_DIM

    def rope(x):
        return x * cos + _swap_halves(x, lane) * sin

    ckv = _dot((xn * gkv_ref[...]).astype(BF16), wdown_ref[...])
    latent = _rms(ckv[:, :kv_rank], glat_ref[...]).astype(BF16)
    kpe_raw = jnp.concatenate([ckv[:, kv_rank:], jnp.zeros((tm, 128 - ROPE_DIM), F32)], axis=1)
    kpe = jnp.where(low, rope(kpe_raw), 0.0).astype(BF16)
    k_nope = _dot(latent, wupk_ref[...])
    v = _dot(latent, wupv_ref[...])
    ones = jnp.ones((tm, HEAD_DIM), BF16)
    for hh in range(n_heads):
        sl = slice(hh * HEAD_DIM, (hh + 1) * HEAD_DIM)
        k_ref[rw, hh * QK_PAD:hh * QK_PAD + HEAD_DIM] = k_nope[:, sl].astype(BF16)
        k_ref[rw, hh * QK_PAD + HEAD_DIM:(hh + 1) * QK_PAD] = kpe
        v_ref[rw, 2 * hh * HEAD_DIM:(2 * hh + 1) * HEAD_DIM] = v[:, sl].astype(BF16)
        v_ref[rw, (2 * hh + 1) * HEAD_DIM:(2 * hh + 2) * HEAD_DIM] = ones

    proj = _dot((xn * gmix_ref[...]).astype(BF16), win_ref[...])
    qm_ref[rw, :] = proj[:, q_rank:]
    cq = _rms(proj[:, :q_rank], gqa_ref[...]).astype(BF16)
    scale = LOG2_E * QK_HEAD_DIM ** -0.5
    q_nope = _dot(cq, wqn_ref[...]) * scale
    q_pe = _dot(cq, wqp_ref[...]) * scale
    for hp in range(n_heads // 2):
        pe = rope(q_pe[:, hp * 128:(hp + 1) * 128])
        for e in range(2):
            hh = 2 * hp + e
            q_ref[rw, hh * QK_PAD:hh * QK_PAD + HEAD_DIM] = q_nope[:, hh * HEAD_DIM:(hh + 1) * HEAD_DIM].astype(BF16)
            pe_h = pe if e == 0 else pltpu.roll(pe, ROPE_DIM, 1)
            q_ref[rw, hh * QK_PAD + HEAD_DIM:(hh + 1) * QK_PAD] = jnp.where(low, pe_h, 0.0).astype(BF16)


def _proj_b(h, pos, freq, g_kv, g_mix, w_down, g_lat, w_upk, w_upv, w_in, g_qa, w_qn, w_qp, tm):
    t, d = h.shape
    kv_rank = w_upk.shape[0]
    q_rank = w_qn.shape[0]
    n_heads = w_upk.shape[1] // HEAD_DIM
    mw = w_in.shape[1] - q_rank
    g_kv, g_mix, g_lat, g_qa = (g_kv.reshape(1, -1), g_mix.reshape(1, -1), g_lat.reshape(1, -1),
                                g_qa.reshape(1, -1))
    consts = (freq, g_kv, g_mix, w_down, g_lat, w_upk, w_upv, w_in, g_qa, w_qn, w_qp)
    kern = functools.partial(_proj_b_kernel, n_heads=n_heads, kv_rank=kv_rank, q_rank=q_rank, rc=256)
    return pl.pallas_call(
        kern,
        grid=(t // tm,),
        in_specs=[
            pl.BlockSpec((tm, d), lambda i: (i, 0)),
            pl.BlockSpec((tm, 1), lambda i: (i, 0)),
        ] + [_const_spec(a) for a in consts],
        out_specs=[
            pl.BlockSpec((tm, n_heads * QK_PAD), lambda i: (i, 0)),
            pl.BlockSpec((tm, n_heads * QK_PAD), lambda i: (i, 0)),
            pl.BlockSpec((tm, 2 * n_heads * HEAD_DIM), lambda i: (i, 0)),
            pl.BlockSpec((tm, mw), lambda i: (i, 0)),
        ],
        out_shape=[
            jax.ShapeDtypeStruct((t, n_heads * QK_PAD), BF16),
            jax.ShapeDtypeStruct((t, n_heads * QK_PAD), BF16),
            jax.ShapeDtypeStruct((t, 2 * n_heads * HEAD_DIM), BF16),
            jax.ShapeDtypeStruct((t, mw), F32),
        ],
        compiler_params=_params(("arbitrary",)),
        name="proj_b",
    )(h, pos, *consts)


def _mla_kernel(q_ref, k_ref, v_ref, o_ref, sa_ref, sb_ref, m_ref, acc_ref, *, tq, nq):
    m_ref[...] = jnp.full_like(m_ref, -1e30)
    acc_ref[...] = jnp.zeros_like(acc_ref)

    def rows(blk):
        return pl.ds(pl.multiple_of(blk * tq, tq), tq)

    def scores(qi, kb, s_ref):
        s_ref[...] = _dot_nt(q_ref[rows(qi), :], k_ref[rows(kb), :])

    def consume(qi, kb, s_ref, diagonal):
        s = s_ref[...]
        if diagonal:
            qc = lax.broadcasted_iota(jnp.int32, (tq, tq), 0) // CHUNK
            kc = lax.broadcasted_iota(jnp.int32, (tq, tq), 1) // CHUNK
            s = jnp.where(kc <= qc, s, -1e30)
        m_old = m_ref[...]
        m_new = jnp.maximum(m_old, jnp.max(s, axis=-1, keepdims=True))
        alpha = jnp.exp2(m_old - m_new)
        p = jnp.exp2(s - jnp.concatenate([m_new] * (tq // 128), axis=1))
        pv = _dot(p.astype(BF16), v_ref[rows(kb), :])
        acc = jnp.concatenate([alpha, alpha], axis=1) * acc_ref[...] + pv
        if diagonal:
            o_ref[rows(qi), :] = (acc[:, :HEAD_DIM] / acc[:, HEAD_DIM:]).astype(BF16)
            m_ref[...] = jnp.full_like(m_ref, -1e30)
            acc_ref[...] = jnp.zeros_like(acc_ref)
        else:
            acc_ref[...] = acc
            m_ref[...] = m_new

    def following(qi, kb):
        is_diag = kb == qi
        return jnp.where(is_diag, qi + 1, qi), jnp.where(is_diag, 0, kb + 1)

    scores(0, 0, sa_ref)

    def pair(_, c):
        q0, k0 = c
        q1, k1 = following(q0, k0)
        q2, k2 = following(q1, k1)
        q2c = jnp.minimum(q2, nq - 1)
        d0, d1 = k0 == q0, k1 == q1

        for v0 in (False, True):
            for v1 in (False, True):
                @pl.when(jnp.logical_and(d0 == v0, d1 == v1))
                def _(v0=v0, v1=v1):
                    scores(q1, k1, sb_ref)
                    consume(q0, k0, sa_ref, v0)
                    scores(q2c, k2, sa_ref)
                    consume(q1, k1, sb_ref, v1)

        return q2, k2

    n_items = nq * (nq + 1) // 2
    lax.fori_loop(0, n_items // 2, pair, (jnp.int32(0), jnp.int32(0)))


def _mla(q, k, v, nb, seq, n_heads, tq):
    t = q.shape[0]
    nq = seq // tq
    assert (nq * (nq + 1) // 2) % 2 == 0
    kern = functools.partial(_mla_kernel, tq=tq, nq=nq)
    return pl.pallas_call(
        kern,
        grid=(nb, n_heads),
        in_specs=[
            pl.BlockSpec((seq, QK_PAD), lambda b, h: (b, h)),
            pl.BlockSpec((seq, QK_PAD), lambda b, h: (b, h)),
            pl.BlockSpec((seq, 2 * HEAD_DIM), lambda b, h: (b, h)),
        ],
        out_specs=pl.BlockSpec((seq, HEAD_DIM), lambda b, h: (b, h)),
        out_shape=jax.ShapeDtypeStruct((t, n_heads * HEAD_DIM), BF16),
        scratch_shapes=[
            pltpu.VMEM((tq, tq), F32),
            pltpu.VMEM((tq, tq), F32),
            pltpu.VMEM((tq, 128), F32),
            pltpu.VMEM((tq, 2 * HEAD_DIM), F32),
        ],
        compiler_params=_params(("arbitrary", "arbitrary")),
        name="mla",
    )(q, k, v)


def kernel(x, mem, positions, g_mix_pre, g_mix_post, g_mlp_pre, g_mlp_post, g_mem, w_mem_k, w_mem_v, w_o, w_ff1, w_ff2, a_w_in, a_conv_w, a_conv_b, a_w_rgate, a_b_rgate, a_w_igate, a_b_igate, a_lambda, b_w_in, b_g_qa, b_w_qb, kv_g_in, kv_w_down, kv_g_latent, kv_w_up):
    nb, seq, d = x.shape
    t = nb * seq
    depth = g_mix_pre.shape[0]
    n_a = a_w_in.shape[0]
    lru_width = a_conv_w.shape[2]
    n_heads = lru_width // HEAD_DIM
    kv_rank = kv_g_latent.shape[0]
    q_rank = b_g_qa.shape[1]
    assert kv_w_down.shape[1] == kv_rank + ROPE_DIM
    assert seq % 512 == 0 and t % 1024 == 0 and d % 128 == 0
    assert depth - n_a == 1, "the shared K/V projection is fused into the single MLA layer's projections"

    bf = lambda w: w.astype(BF16)
    mk, mv = _mem_kv(mem, g_mem, bf(w_mem_k), bf(w_mem_v))
    w_o16, w_ff1_16, w_ff2_16 = bf(w_o), bf(w_ff1), bf(w_ff2)

    half = ROPE_DIM // 2
    inv_freq = ROPE_THETA ** (-jnp.arange(half, dtype=F32) / half)
    freq = jnp.tile(inv_freq, 128 // half).reshape(1, 128)
    pos = positions.reshape(t, 1)

    w_up = bf(kv_w_up).reshape(kv_rank, n_heads, 2, HEAD_DIM)
    w_upk = w_up[:, :, 0, :].reshape(kv_rank, n_heads * HEAD_DIM)
    w_upv = w_up[:, :, 1, :].reshape(kv_rank, n_heads * HEAD_DIM)

    h = x.reshape(t, d)
    for layer in range(depth):
        if layer < n_a:
            la = layer
            wg = _pair_gate_weights(bf(a_w_rgate[la]), bf(a_w_igate[la]))
            tok, qm = _lru_layer(h, g_mix_pre[layer], bf(a_w_in[la]), nb, seq, a_conv_w[la], a_conv_b[la], wg,
                                 a_b_rgate[la], a_b_igate[la], a_lambda[la], ts=512)
        else:
            lb = layer - n_a
            w_qb = bf(b_w_qb[lb]).reshape(q_rank, n_heads, QK_HEAD_DIM)
            w_qn = w_qb[:, :, :HEAD_DIM].reshape(q_rank, n_heads * HEAD_DIM)
            w_qp = w_qb[:, :, HEAD_DIM:].reshape(q_rank, n_heads * ROPE_DIM)
            q, k, v, qm = _proj_b(h, pos, freq, kv_g_in, g_mix_pre[layer], bf(kv_w_down), kv_g_latent,
                                  w_upk, w_upv, bf(b_w_in[lb]), b_g_qa[lb], w_qn, w_qp, tm=512)
            tok = _mla(q, k, v, nb, seq, n_heads, tq=512)
        h = _mix_out(tok, qm, mk, mv, layer, w_o16, h, g_mix_post[layer], seq, tm=512)
        h = _mlp(h, g_mlp_pre[layer], w_ff1_16, w_ff2_16, layer, g_mlp_post[layer], tm=1024, tf=512)
    return h.reshape(nb, seq, d)
```

```python
import functools

import jax
import jax.numpy as jnp
from jax import lax
from jax.experimental import pallas as pl
from jax.experimental.pallas import tpu as pltpu

EPS = 1e-6
LRU_C = 8.0
CHUNK = 64
ROPE_THETA = 10000.0
LOG2_E = 1.4426950408889634
F32_TINY = 1.1754943508222875e-38
MEM_HEADS = 4
HEAD_DIM = 128
ROPE_DIM = 64
QK_HEAD_DIM = HEAD_DIM + ROPE_DIM
QK_PAD = 256
SUBLANES = 8
VMEM_LIMIT = 56 * 1024 * 1024

BF16 = jnp.bfloat16
F32 = jnp.float32


def _params(sem, vmem=VMEM_LIMIT):
    return pltpu.CompilerParams(dimension_semantics=sem, vmem_limit_bytes=vmem)


def _const_spec(a):
    return pl.BlockSpec(a.shape, lambda *_: (0,) * a.ndim, pipeline_mode=pl.Buffered(1))


def _rms(x, g):
    return x * lax.rsqrt(jnp.mean(x * x, axis=-1, keepdims=True) + EPS) * g


def _dot(a, b):
    return jnp.dot(a, b, preferred_element_type=F32)


def _dot_nt(a, b):
    return lax.dot_general(a, b, (((1,), (1,)), ((), ())), preferred_element_type=F32)


def _mem_kv_kernel(mem_ref, g_ref, wk_ref, wv_ref, mk_ref, mv_ref):
    mn = _rms(mem_ref[0], g_ref[0]).astype(BF16)
    mk_ref[0, 0] = _dot(mn, wk_ref[0]).astype(BF16)
    mv = _dot(mn, wv_ref[0]).astype(BF16)
    ones = jnp.ones((mv.shape[0], HEAD_DIM), BF16)
    for hh in range(MEM_HEADS):
        mv_ref[0, 0, :, 2 * hh * HEAD_DIM:(2 * hh + 1) * HEAD_DIM] = mv[:, hh * HEAD_DIM:(hh + 1) * HEAD_DIM]
        mv_ref[0, 0, :, (2 * hh + 1) * HEAD_DIM:(2 * hh + 2) * HEAD_DIM] = ones


def _mem_kv(mem, g_mem, w_k, w_v):
    nb, ml, d = mem.shape
    nl, _, mw = w_k.shape
    assert mw == MEM_HEADS * HEAD_DIM
    return pl.pallas_call(
        _mem_kv_kernel,
        grid=(nl, nb),
        in_specs=[
            pl.BlockSpec((1, ml, d), lambda l, b: (b, 0, 0)),
            pl.BlockSpec((1, 1, d), lambda l, b: (l, 0, 0)),
            pl.BlockSpec((1, d, mw), lambda l, b: (l, 0, 0)),
            pl.BlockSpec((1, d, mw), lambda l, b: (l, 0, 0)),
        ],
        out_specs=[
            pl.BlockSpec((1, 1, ml, mw), lambda l, b: (l, b, 0, 0)),
            pl.BlockSpec((1, 1, ml, 2 * mw), lambda l, b: (l, b, 0, 0)),
        ],
        out_shape=[jax.ShapeDtypeStruct((nl, nb, ml, mw), BF16),
                   jax.ShapeDtypeStruct((nl, nb, ml, 2 * mw), BF16)],
        compiler_params=_params(("arbitrary", "arbitrary")),
        name="mem_kv",
    )(mem, g_mem.reshape(nl, 1, d), w_k, w_v)


def _lru_layer_kernel(x_ref, g_ref, w_ref, cw_ref, cb_ref, wg_ref, br_ref, bi_ref, lam_ref,
                      tok_ref, qm_ref, hn_ref, tail_ref, carry_ref, *, ts, conv_width):
    c = cw_ref.shape[1]
    pw = 2 * HEAD_DIM

    @pl.when(pl.program_id(1) == 0)
    def _():
        tail_ref[...] = jnp.zeros_like(tail_ref)
        carry_ref[...] = jnp.zeros_like(carry_ref)

    hn_ref[...] = _rms(x_ref[...], g_ref[...]).astype(BF16)
    neg_c_sp = -LRU_C * jax.nn.softplus(-lam_ref[...])
    row = lax.broadcasted_iota(jnp.int32, (SUBLANES, pw), 0)

    def sigmoid(z):
        return 0.5 * jnp.tanh(0.5 * z) + 0.5

    def x_branch(p):
        return _dot(hn_ref[...], w_ref[:, p * pw:(p + 1) * pw])

    xb_next = x_branch(0)
    for p in range(c // pw):
        sl = slice(p * pw, (p + 1) * pw)
        xb = xb_next
        if p + 1 < c // pw:
            xb_next = x_branch(p + 1)
        tail = tail_ref[:, sl]
        xc = cb_ref[:, sl] + cw_ref[conv_width - 1:conv_width, sl] * xb
        for j in range(1, conv_width):
            shifted = pltpu.roll(xb, j, 0)
            top = jnp.where(row < j, pltpu.roll(tail, j, 0), shifted[:SUBLANES, :])
            shifted = jnp.concatenate([top, shifted[SUBLANES:, :]], axis=0)
            xc = xc + cw_ref[conv_width - 1 - j:conv_width - j, sl] * shifted
        tail_ref[:, sl] = xb[ts - SUBLANES:, :]
        ri = _dot(xc.astype(BF16), wg_ref[p])
        r = sigmoid(ri[:, :pw] + br_ref[:, sl])
        i = sigmoid(ri[:, pw:] + bi_ref[:, sl])
        log_a = r * neg_c_sp[:, sl]
        a_all = jnp.exp(log_a)
        y = jnp.tanh(-log_a) * (1.0 + a_all * a_all)
        b_all = (y * lax.rsqrt(jnp.maximum(y, F32_TINY))) * (i * xc)

        carry = carry_ref[:, sl]
        hs = []
        for g in range(ts // SUBLANES):
            a = a_all[g * SUBLANES:(g + 1) * SUBLANES, :]
            b = b_all[g * SUBLANES:(g + 1) * SUBLANES, :]
            for d in (1, 2, 4):
                keep = row >= d
                a_sh = pltpu.roll(a, d, 0)
                b_sh = pltpu.roll(b, d, 0)
                b = jnp.where(keep, a * b_sh + b, b)
                a = jnp.where(keep, a * a_sh, a)
            h = b + a * carry
            hs.append(h)
            carry = jnp.broadcast_to(h[SUBLANES - 1:SUBLANES, :], (SUBLANES, pw))
        carry_ref[:, sl] = carry

        gate = _dot(hn_ref[...], w_ref[:, c + p * pw:c + (p + 1) * pw])
        tok_ref[:, sl] = (jnp.concatenate(hs, axis=0) * jax.nn.gelu(gate)).astype(BF16)
    qm_ref[...] = _dot(hn_ref[...], w_ref[:, 2 * c:])


def _lru_layer(x, g, w_in, nb, seq, cw, cb, wg, br, bi, lam, ts):
    t, d = x.shape
    c = cw.shape[1]
    mw = w_in.shape[1] - 2 * c
    spb = seq // ts
    g, cb, br, bi, lam = (v.reshape(1, -1) for v in (g, cb, br, bi, lam))
    kern = functools.partial(_lru_layer_kernel, ts=ts, conv_width=cw.shape[0])
    return pl.pallas_call(
        kern,
        grid=(nb, spb),
        in_specs=[pl.BlockSpec((ts, d), lambda b, s: (b * spb + s, 0))]
                 + [_const_spec(v) for v in (g, w_in, cw, cb, wg, br, bi, lam)],
        out_specs=[
            pl.BlockSpec((ts, c), lambda b, s: (b * spb + s, 0)),
            pl.BlockSpec((ts, mw), lambda b, s: (b * spb + s, 0)),
        ],
        out_shape=[jax.ShapeDtypeStruct((t, c), BF16), jax.ShapeDtypeStruct((t, mw), F32)],
        scratch_shapes=[
            pltpu.VMEM((ts, d), BF16),
            pltpu.VMEM((SUBLANES, c), F32),
            pltpu.VMEM((SUBLANES, c), F32),
        ],
        compiler_params=_params(("arbitrary", "arbitrary")),
        name="lru_layer",
    )(x, g, w_in, cw, cb, wg, br, bi, lam)


def _pair_gate_weights(w_r, w_i):
    n, bd, _ = w_r.shape
    z = jnp.zeros((n // 2, bd, bd), w_r.dtype)

    def blockdiag(w):
        w = w.reshape(n // 2, 2, bd, bd)
        top = jnp.concatenate([w[:, 0], z], axis=2)
        bot = jnp.concatenate([z, w[:, 1]], axis=2)
        return jnp.concatenate([top, bot], axis=1)

    return jnp.concatenate([blockdiag(w_r), blockdiag(w_i)], axis=2)


def _mix_out_kernel(tok_ref, qm_ref, mk_ref, mv_ref, wo_ref, h_ref, g_ref, o_ref, *, tok_width, rc):
    scale = LOG2_E * HEAD_DIM ** -0.5
    for r in range(tok_ref.shape[0] // rc):
        rw = pl.ds(r * rc, rc)
        y = _dot(tok_ref[rw, :], wo_ref[0:tok_width, :])
        heads = []
        for hh in range(MEM_HEADS):
            sl = slice(hh * HEAD_DIM, (hh + 1) * HEAD_DIM)
            q = (qm_ref[rw, sl] * scale).astype(BF16)
            s = _dot_nt(q, mk_ref[0, 0, :, sl])
            p = jnp.exp2(s - jnp.max(s, axis=-1, keepdims=True))
            pv = _dot(p.astype(BF16), mv_ref[0, 0, :, 2 * hh * HEAD_DIM:(2 * hh + 2) * HEAD_DIM])
            heads.append((pv[:, :HEAD_DIM] / pv[:, HEAD_DIM:]).astype(BF16))
        y = y + _dot(jnp.concatenate(heads, axis=1), wo_ref[tok_width:, :])
        o_ref[rw, :] = h_ref[rw, :] + _rms(y, g_ref[...])


def _mix_out(tok, qm, mk, mv, layer, w_o, h, g_post, seq, tm):
    t, tok_width = tok.shape
    d = h.shape[1]
    ml, mw = mk.shape[2], mk.shape[3]
    bpb = seq // tm
    kern = functools.partial(_mix_out_kernel, tok_width=tok_width, rc=tm)
    return pl.pallas_call(
        kern,
        grid=(t // tm,),
        in_specs=[
            pl.BlockSpec((tm, tok_width), lambda i: (i, 0)),
            pl.BlockSpec((tm, mw), lambda i: (i, 0)),
            pl.BlockSpec((1, 1, ml, mw), lambda i: (layer, i // bpb, 0, 0)),
            pl.BlockSpec((1, 1, ml, 2 * mw), lambda i: (layer, i // bpb, 0, 0)),
            pl.BlockSpec((None, tok_width + mw, d), lambda i: (layer, 0, 0), pipeline_mode=pl.Buffered(1)),
            pl.BlockSpec((tm, d), lambda i: (i, 0)),
            pl.BlockSpec((1, d), lambda i: (0, 0)),
        ],
        out_specs=pl.BlockSpec((tm, d), lambda i: (i, 0)),
        out_shape=jax.ShapeDtypeStruct((t, d), F32),
        compiler_params=_params(("arbitrary",)),
        name="mix_out",
    )(tok, qm, mk, mv, w_o, h, g_post.reshape(1, d))


def _mlp_kernel(h_ref, gpre_ref, w1_ref, w2_ref, gpost_ref, o_ref, hn_ref, *, rc):
    j = pl.program_id(1)
    last = pl.num_programs(1) - 1
    tm = h_ref.shape[0]

    def ffn(hn):
        f = _dot(hn, w1_ref[...])
        return _dot(jnp.square(jnp.maximum(f, 0.0)).astype(BF16), w2_ref[...])

    @pl.when(j == 0)
    def _():
        for r in range(tm // rc):
            rows = pl.ds(r * rc, rc)
            hn = _rms(h_ref[rows, :], gpre_ref[...]).astype(BF16)
            hn_ref[rows, :] = hn
            o_ref[rows, :] = ffn(hn)

    @pl.when(jnp.logical_and(j > 0, j < last))
    def _():
        o_ref[...] += ffn(hn_ref[...])

    @pl.when(j == last)
    def _():
        for r in range(tm // rc):
            rows = pl.ds(r * rc, rc)
            acc = o_ref[rows, :] + ffn(hn_ref[rows, :])
            o_ref[rows, :] = h_ref[rows, :] + _rms(acc, gpost_ref[...])


def _mlp(h, g_pre, w1, w2, layer, g_post, tm, tf):
    t, d = h.shape
    dff = w1.shape[2]
    assert dff // tf >= 2
    return pl.pallas_call(
        functools.partial(_mlp_kernel, rc=512),
        grid=(t // tm, dff // tf),
        in_specs=[
            pl.BlockSpec((tm, d), lambda i, j: (i, 0)),
            pl.BlockSpec((1, d), lambda i, j: (0, 0)),
            pl.BlockSpec((None, d, tf), lambda i, j: (layer, 0, j)),
            pl.BlockSpec((None, tf, d), lambda i, j: (layer, j, 0)),
            pl.BlockSpec((1, d), lambda i, j: (0, 0)),
        ],
        out_specs=pl.BlockSpec((tm, d), lambda i, j: (i, 0)),
        out_shape=jax.ShapeDtypeStruct((t, d), F32),
        scratch_shapes=[pltpu.VMEM((tm, d), BF16)],
        compiler_params=_params(("arbitrary", "arbitrary")),
        name="mlp",
    )(h, g_pre.reshape(1, d), w1, w2, g_post.reshape(1, d))


def _swap_halves(x, lane):
    return jnp.where(lane % ROPE_DIM < ROPE_DIM // 2, pltpu.roll(x, 128 - ROPE_DIM // 2, 1),
                     pltpu.roll(x, ROPE_DIM // 2, 1))


def _proj_b_kernel(h_ref, pos_ref, freq_ref, gkv_ref, gmix_ref, wdown_ref, glat_ref, wupk_ref, wupv_ref,
                   win_ref, gqa_ref, wqn_ref, wqp_ref,
                   q_ref, k_ref, v_ref, qm_ref, *, n_heads, kv_rank, q_rank, rc):
    for r in range(h_ref.shape[0] // rc):
        _proj_b_rows(pl.ds(r * rc, rc), rc, h_ref, pos_ref, freq_ref, gkv_ref, gmix_ref, wdown_ref, glat_ref,
                     wupk_ref, wupv_ref, win_ref, gqa_ref, wqn_ref, wqp_ref, q_ref, k_ref, v_ref, qm_ref,
                     n_heads, kv_rank, q_rank)


def _proj_b_rows(rw, tm, h_ref, pos_ref, freq_ref, gkv_ref, gmix_ref, wdown_ref, glat_ref, wupk_ref, wupv_ref,
                 win_ref, gqa_ref, wqn_ref, wqp_ref, q_ref, k_ref, v_ref, qm_ref, n_heads, kv_rank, q_rank):
    h = h_ref[rw, :]
    xn = h * lax.rsqrt(jnp.mean(h * h, axis=-1, keepdims=True) + EPS)

    lane = lax.broadcasted_iota(jnp.int32, (tm, 128), 1)
    ang = pos_ref[rw, :].astype(F32) * freq_ref[...]
    cos = jnp.cos(ang)
    sin = jnp.where(lane % ROPE_DIM < ROPE_DIM // 2, -jnp.sin(ang), jnp.sin(ang))
    low = lane < ROPE_DIM

    def rope(x):
        return x * cos + _swap_halves(x, lane) * sin

    ckv = _dot((xn * gkv_ref[...]).astype(BF16), wdown_ref[...])
    latent = _rms(ckv[:, :kv_rank], glat_ref[...]).astype(BF16)
    kpe_raw = jnp.concatenate([ckv[:, kv_rank:], jnp.zeros((tm, 128 - ROPE_DIM), F32)], axis=1)
    kpe = jnp.where(low, rope(kpe_raw), 0.0).astype(BF16)
    k_nope = _dot(latent, wupk_ref[...])
    v = _dot(latent, wupv_ref[...])
    ones = jnp.ones((tm, HEAD_DIM), BF16)
    for hh in range(n_heads):
        sl = slice(hh * HEAD_DIM, (hh + 1) * HEAD_DIM)
        k_ref[rw, hh * QK_PAD:hh * QK_PAD + HEAD_DIM] = k_nope[:, sl].astype(BF16)
        k_ref[rw, hh * QK_PAD + HEAD_DIM:(hh + 1) * QK_PAD] = kpe
        v_ref[rw, 2 * hh * HEAD_DIM:(2 * hh + 1) * HEAD_DIM] = v[:, sl].astype(BF16)
        v_ref[rw, (2 * hh + 1) * HEAD_DIM:(2 * hh + 2) * HEAD_DIM] = ones

    proj = _dot((xn * gmix_ref[...]).astype(BF16), win_ref[...])
    qm_ref[rw, :] = proj[:, q_rank:]
    cq = _rms(proj[:, :q_rank], gqa_ref[...]).astype(BF16)
    scale = LOG2_E * QK_HEAD_DIM ** -0.5
    q_nope = _dot(cq, wqn_ref[...]) * scale
    q_pe = _dot(cq, wqp_ref[...]) * scale
    for hp in range(n_heads // 2):
        pe = rope(q_pe[:, hp * 128:(hp + 1) * 128])
        for e in range(2):
            hh = 2 * hp + e
            q_ref[rw, hh * QK_PAD:hh * QK_PAD + HEAD_DIM] = q_nope[:, hh * HEAD_DIM:(hh + 1) * HEAD_DIM].astype(BF16)
            pe_h = pe if e == 0 else pltpu.roll(pe, ROPE_DIM, 1)
            q_ref[rw, hh * QK_PAD + HEAD_DIM:(hh + 1) * QK_PAD] = jnp.where(low, pe_h, 0.0).astype(BF16)


def _proj_b(h, pos, freq, g_kv, g_mix, w_down, g_lat, w_upk, w_upv, w_in, g_qa, w_qn, w_qp, tm):
    t, d = h.shape
    kv_rank = w_upk.shape[0]
    q_rank = w_qn.shape[0]
    n_heads = w_upk.shape[1] // HEAD_DIM
    mw = w_in.shape[1] - q_rank
    g_kv, g_mix, g_lat, g_qa = (g_kv.reshape(1, -1), g_mix.reshape(1, -1), g_lat.reshape(1, -1),
                                g_qa.reshape(1, -1))
    consts = (freq, g_kv, g_mix, w_down, g_lat, w_upk, w_upv, w_in, g_qa, w_qn, w_qp)
    kern = functools.partial(_proj_b_kernel, n_heads=n_heads, kv_rank=kv_rank, q_rank=q_rank, rc=256)
    return pl.pallas_call(
        kern,
        grid=(t // tm,),
        in_specs=[
            pl.BlockSpec((tm, d), lambda i: (i, 0)),
            pl.BlockSpec((tm, 1), lambda i: (i, 0)),
        ] + [_const_spec(a) for a in consts],
        out_specs=[
            pl.BlockSpec((tm, n_heads * QK_PAD), lambda i: (i, 0)),
            pl.BlockSpec((tm, n_heads * QK_PAD), lambda i: (i, 0)),
            pl.BlockSpec((tm, 2 * n_heads * HEAD_DIM), lambda i: (i, 0)),
            pl.BlockSpec((tm, mw), lambda i: (i, 0)),
        ],
        out_shape=[
            jax.ShapeDtypeStruct((t, n_heads * QK_PAD), BF16),
            jax.ShapeDtypeStruct((t, n_heads * QK_PAD), BF16),
            jax.ShapeDtypeStruct((t, 2 * n_heads * HEAD_DIM), BF16),
            jax.ShapeDtypeStruct((t, mw), F32),
        ],
        compiler_params=_params(("arbitrary",)),
        name="proj_b",
    )(h, pos, *consts)


def _mla_kernel(q_ref, k_ref, v_ref, o_ref, sa_ref, sb_ref, m_ref, acc_ref, *, tq, nq, hps):
    m_ref[...] = jnp.full_like(m_ref, -1e30)
    acc_ref[...] = jnp.zeros_like(acc_ref)

    def rows(blk):
        return pl.ds(pl.multiple_of(blk * tq, tq), tq)

    def scores(hd, qi, kb, s_ref):
        qk = slice(hd * QK_PAD, (hd + 1) * QK_PAD)
        s_ref[hd] = _dot_nt(q_ref[rows(qi), qk], k_ref[rows(kb), qk])

    def consume(hd, qi, kb, s_ref, diagonal):
        s = s_ref[hd]
        if diagonal:
            qc = lax.broadcasted_iota(jnp.int32, (tq, tq), 0) // CHUNK
            kc = lax.broadcasted_iota(jnp.int32, (tq, tq), 1) // CHUNK
            s = jnp.where(kc <= qc, s, -1e30)
        m_old = m_ref[hd]
        m_new = jnp.maximum(m_old, jnp.max(s, axis=-1, keepdims=True))
        alpha = jnp.exp2(m_old - m_new)
        p = jnp.exp2(s - jnp.concatenate([m_new] * (tq // 128), axis=1))
        pv = _dot(p.astype(BF16), v_ref[rows(kb), 2 * hd * HEAD_DIM:2 * (hd + 1) * HEAD_DIM])
        acc = jnp.concatenate([alpha, alpha], axis=1) * acc_ref[hd] + pv
        if diagonal:
            o_ref[rows(qi), hd * HEAD_DIM:(hd + 1) * HEAD_DIM] = (acc[:, :HEAD_DIM] / acc[:, HEAD_DIM:]).astype(BF16)
            m_ref[hd] = jnp.full((tq, 128), -1e30, F32)
            acc_ref[hd] = jnp.zeros((tq, 2 * HEAD_DIM), F32)
        else:
            acc_ref[hd] = acc
            m_ref[hd] = m_new

    def following(qi, kb):
        is_diag = kb == qi
        return jnp.where(is_diag, qi + 1, qi), jnp.where(is_diag, 0, kb + 1)

    for hd in range(hps):
        scores(hd, 0, 0, sa_ref)

    def pair(_, c):
        q0, k0 = c
        q1, k1 = following(q0, k0)
        q2, k2 = following(q1, k1)
        q2c = jnp.minimum(q2, nq - 1)
        d0, d1 = k0 == q0, k1 == q1

        for v0 in (False, True):
            for v1 in (False, True):
                @pl.when(jnp.logical_and(d0 == v0, d1 == v1))
                def _(v0=v0, v1=v1):
                    for hd in range(hps):
                        scores(hd, q1, k1, sb_ref)
                    for hd in range(hps):
                        consume(hd, q0, k0, sa_ref, v0)
                    for hd in range(hps):
                        scores(hd, q2c, k2, sa_ref)
                    for hd in range(hps):
                        consume(hd, q1, k1, sb_ref, v1)

        return q2, k2

    n_items = nq * (nq + 1) // 2
    lax.fori_loop(0, n_items // 2, pair, (jnp.int32(0), jnp.int32(0)))


def _mla(q, k, v, nb, seq, n_heads, tq, hps):
    t = q.shape[0]
    nq = seq // tq
    assert (nq * (nq + 1) // 2) % 2 == 0
    assert n_heads % hps == 0
    kern = functools.partial(_mla_kernel, tq=tq, nq=nq, hps=hps)
    return pl.pallas_call(
        kern,
        grid=(nb, n_heads // hps),
        in_specs=[
            pl.BlockSpec((seq, hps * QK_PAD), lambda b, h: (b, h)),
            pl.BlockSpec((seq, hps * QK_PAD), lambda b, h: (b, h)),
            pl.BlockSpec((seq, hps * 2 * HEAD_DIM), lambda b, h: (b, h)),
        ],
        out_specs=pl.BlockSpec((seq, hps * HEAD_DIM), lambda b, h: (b, h)),
        out_shape=jax.ShapeDtypeStruct((t, n_heads * HEAD_DIM), BF16),
        scratch_shapes=[
            pltpu.VMEM((hps, tq, tq), F32),
            pltpu.VMEM((hps, tq, tq), F32),
            pltpu.VMEM((hps, tq, 128), F32),
            pltpu.VMEM((hps, tq, 2 * HEAD_DIM), F32),
        ],
        compiler_params=_params(("arbitrary", "arbitrary")),
        name="mla",
    )(q, k, v)


def kernel(x, mem, positions, g_mix_pre, g_mix_post, g_mlp_pre, g_mlp_post, g_mem, w_mem_k, w_mem_v, w_o, w_ff1, w_ff2, a_w_in, a_conv_w, a_conv_b, a_w_rgate, a_b_rgate, a_w_igate, a_b_igate, a_lambda, b_w_in, b_g_qa, b_w_qb, kv_g_in, kv_w_down, kv_g_latent, kv_w_up):
    nb, seq, d = x.shape
    t = nb * seq
    depth = g_mix_pre.shape[0]
    n_a = a_w_in.shape[0]
    lru_width = a_conv_w.shape[2]
    n_heads = lru_width // HEAD_DIM
    kv_rank = kv_g_latent.shape[0]
    q_rank = b_g_qa.shape[1]
    assert kv_w_down.shape[1] == kv_rank + ROPE_DIM
    assert seq % 512 == 0 and t % 1024 == 0 and d % 128 == 0
    assert depth - n_a == 1, "the shared K/V projection is fused into the single MLA layer's projections"

    bf = lambda w: w.astype(BF16)
    mk, mv = _mem_kv(mem, g_mem, bf(w_mem_k), bf(w_mem_v))
    w_o16, w_ff1_16, w_ff2_16 = bf(w_o), bf(w_ff1), bf(w_ff2)

    half = ROPE_DIM // 2
    inv_freq = ROPE_THETA ** (-jnp.arange(half, dtype=F32) / half)
    freq = jnp.tile(inv_freq, 128 // half).reshape(1, 128)
    pos = positions.reshape(t, 1)

    w_up = bf(kv_w_up).reshape(kv_rank, n_heads, 2, HEAD_DIM)
    w_upk = w_up[:, :, 0, :].reshape(kv_rank, n_heads * HEAD_DIM)
    w_upv = w_up[:, :, 1, :].reshape(kv_rank, n_heads * HEAD_DIM)

    h = x.reshape(t, d)
    for layer in range(depth):
        if layer < n_a:
            la = layer
            wg = _pair_gate_weights(bf(a_w_rgate[la]), bf(a_w_igate[la]))
            tok, qm = _lru_layer(h, g_mix_pre[layer], bf(a_w_in[la]), nb, seq, a_conv_w[la], a_conv_b[la], wg,
                                 a_b_rgate[la], a_b_igate[la], a_lambda[la], ts=512)
        else:
            lb = layer - n_a
            w_qb = bf(b_w_qb[lb]).reshape(q_rank, n_heads, QK_HEAD_DIM)
            w_qn = w_qb[:, :, :HEAD_DIM].reshape(q_rank, n_heads * HEAD_DIM)
            w_qp = w_qb[:, :, HEAD_DIM:].reshape(q_rank, n_heads * ROPE_DIM)
            q, k, v, qm = _proj_b(h, pos, freq, kv_g_in, g_mix_pre[layer], bf(kv_w_down), kv_g_latent,
                                  w_upk, w_upv, bf(b_w_in[lb]), b_g_qa[lb], w_qn, w_qp, tm=512)
            tok = _mla(q, k, v, nb, seq, n_heads, tq=512, hps=2)
        h = _mix_out(tok, qm, mk, mv, layer, w_o16, h, g_mix_post[layer], seq, tm=512)
        h = _mlp(h, g_mlp_pre[layer], w_ff1_16, w_ff2_16, layer, g_mlp_post[layer], tm=1024, tf=512)
    return h.reshape(nb, seq, d)
```

```python
import functools

import jax
import jax.numpy as jnp
from jax import lax
from jax.experimental import pallas as pl
from jax.experimental.pallas import tpu as pltpu

EPS = 1e-6
LRU_C = 8.0
CHUNK = 64
ROPE_THETA = 10000.0
LOG2_E = 1.4426950408889634
F32_TINY = 1.1754943508222875e-38
MEM_HEADS = 4
HEAD_DIM = 128
ROPE_DIM = 64
QK_HEAD_DIM = HEAD_DIM + ROPE_DIM
QK_PAD = 256
SUBLANES = 8
VMEM_LIMIT = 56 * 1024 * 1024
MLP_VMEM_LIMIT = 61 * 1024 * 1024

BF16 = jnp.bfloat16
F32 = jnp.float32


def _params(sem, vmem=VMEM_LIMIT):
    return pltpu.CompilerParams(dimension_semantics=sem, vmem_limit_bytes=vmem)


def _const_spec(a):
    return pl.BlockSpec(a.shape, lambda *_: (0,) * a.ndim, pipeline_mode=pl.Buffered(1))


def _rms(x, g):
    return x * lax.rsqrt(jnp.mean(x * x, axis=-1, keepdims=True) + EPS) * g


def _dot(a, b):
    return jnp.dot(a, b, preferred_element_type=F32)


def _dot_nt(a, b):
    return lax.dot_general(a, b, (((1,), (1,)), ((), ())), preferred_element_type=F32)


def _mem_kv_kernel(mem_ref, g_ref, wk_ref, wv_ref, mk_ref, mv_ref):
    mn = _rms(mem_ref[0], g_ref[0]).astype(BF16)
    mk_ref[0, 0] = _dot(mn, wk_ref[0]).astype(BF16)
    mv = _dot(mn, wv_ref[0]).astype(BF16)
    ones = jnp.ones((mv.shape[0], HEAD_DIM), BF16)
    for hh in range(MEM_HEADS):
        mv_ref[0, 0, :, 2 * hh * HEAD_DIM:(2 * hh + 1) * HEAD_DIM] = mv[:, hh * HEAD_DIM:(hh + 1) * HEAD_DIM]
        mv_ref[0, 0, :, (2 * hh + 1) * HEAD_DIM:(2 * hh + 2) * HEAD_DIM] = ones


def _mem_kv(mem, g_mem, w_k, w_v):
    nb, ml, d = mem.shape
    nl, _, mw = w_k.shape
    assert mw == MEM_HEADS * HEAD_DIM
    return pl.pallas_call(
        _mem_kv_kernel,
        grid=(nl, nb),
        in_specs=[
            pl.BlockSpec((1, ml, d), lambda l, b: (b, 0, 0)),
            pl.BlockSpec((1, 1, d), lambda l, b: (l, 0, 0)),
            pl.BlockSpec((1, d, mw), lambda l, b: (l, 0, 0)),
            pl.BlockSpec((1, d, mw), lambda l, b: (l, 0, 0)),
        ],
        out_specs=[
            pl.BlockSpec((1, 1, ml, mw), lambda l, b: (l, b, 0, 0)),
            pl.BlockSpec((1, 1, ml, 2 * mw), lambda l, b: (l, b, 0, 0)),
        ],
        out_shape=[jax.ShapeDtypeStruct((nl, nb, ml, mw), BF16),
                   jax.ShapeDtypeStruct((nl, nb, ml, 2 * mw), BF16)],
        compiler_params=_params(("arbitrary", "arbitrary")),
        name="mem_kv",
    )(mem, g_mem.reshape(nl, 1, d), w_k, w_v)


def _lru_layer_kernel(x_ref, g_ref, w_ref, cw_ref, cb_ref, wg_ref, br_ref, bi_ref, lam_ref,
                      tok_ref, qm_ref, hn_ref, tail_ref, carry_ref, *, ts, conv_width):
    c = cw_ref.shape[1]
    pw = 2 * HEAD_DIM

    @pl.when(pl.program_id(1) == 0)
    def _():
        tail_ref[...] = jnp.zeros_like(tail_ref)
        carry_ref[...] = jnp.zeros_like(carry_ref)

    hn_ref[...] = _rms(x_ref[...], g_ref[...]).astype(BF16)
    neg_c_sp = -LRU_C * jax.nn.softplus(-lam_ref[...])
    row = lax.broadcasted_iota(jnp.int32, (SUBLANES, pw), 0)

    def sigmoid(z):
        return 0.5 * jnp.tanh(0.5 * z) + 0.5

    def x_branch(p):
        return _dot(hn_ref[...], w_ref[:, p * pw:(p + 1) * pw])

    xb_next = x_branch(0)
    for p in range(c // pw):
        sl = slice(p * pw, (p + 1) * pw)
        xb = xb_next
        if p + 1 < c // pw:
            xb_next = x_branch(p + 1)
        tail = tail_ref[:, sl]
        xc = cb_ref[:, sl] + cw_ref[conv_width - 1:conv_width, sl] * xb
        for j in range(1, conv_width):
            shifted = pltpu.roll(xb, j, 0)
            top = jnp.where(row < j, pltpu.roll(tail, j, 0), shifted[:SUBLANES, :])
            shifted = jnp.concatenate([top, shifted[SUBLANES:, :]], axis=0)
            xc = xc + cw_ref[conv_width - 1 - j:conv_width - j, sl] * shifted
        tail_ref[:, sl] = xb[ts - SUBLANES:, :]
        ri = _dot(xc.astype(BF16), wg_ref[p])
        r = sigmoid(ri[:, :pw] + br_ref[:, sl])
        i = sigmoid(ri[:, pw:] + bi_ref[:, sl])
        log_a = r * neg_c_sp[:, sl]
        a_all = jnp.exp(log_a)
        y = jnp.tanh(-log_a) * (1.0 + a_all * a_all)
        b_all = (y * lax.rsqrt(jnp.maximum(y, F32_TINY))) * (i * xc)

        carry = carry_ref[:, sl]
        hs = []
        for g in range(ts // SUBLANES):
            a = a_all[g * SUBLANES:(g + 1) * SUBLANES, :]
            b = b_all[g * SUBLANES:(g + 1) * SUBLANES, :]
            for d in (1, 2, 4):
                keep = row >= d
                a_sh = pltpu.roll(a, d, 0)
                b_sh = pltpu.roll(b, d, 0)
                b = jnp.where(keep, a * b_sh + b, b)
                a = jnp.where(keep, a * a_sh, a)
            h = b + a * carry
            hs.append(h)
            carry = jnp.broadcast_to(h[SUBLANES - 1:SUBLANES, :], (SUBLANES, pw))
        carry_ref[:, sl] = carry

        gate = _dot(hn_ref[...], w_ref[:, c + p * pw:c + (p + 1) * pw])
        tok_ref[:, sl] = (jnp.concatenate(hs, axis=0) * jax.nn.gelu(gate)).astype(BF16)
    qm_ref[...] = _dot(hn_ref[...], w_ref[:, 2 * c:])


def _lru_layer(x, g, w_in, nb, seq, cw, cb, wg, br, bi, lam, ts):
    t, d = x.shape
    c = cw.shape[1]
    mw = w_in.shape[1] - 2 * c
    spb = seq // ts
    g, cb, br, bi, lam = (v.reshape(1, -1) for v in (g, cb, br, bi, lam))
    kern = functools.partial(_lru_layer_kernel, ts=ts, conv_width=cw.shape[0])
    return pl.pallas_call(
        kern,
        grid=(nb, spb),
        in_specs=[pl.BlockSpec((ts, d), lambda b, s: (b * spb + s, 0))]
                 + [_const_spec(v) for v in (g, w_in, cw, cb, wg, br, bi, lam)],
        out_specs=[
            pl.BlockSpec((ts, c), lambda b, s: (b * spb + s, 0)),
            pl.BlockSpec((ts, mw), lambda b, s: (b * spb + s, 0)),
        ],
        out_shape=[jax.ShapeDtypeStruct((t, c), BF16), jax.ShapeDtypeStruct((t, mw), F32)],
        scratch_shapes=[
            pltpu.VMEM((ts, d), BF16),
            pltpu.VMEM((SUBLANES, c), F32),
            pltpu.VMEM((SUBLANES, c), F32),
        ],
        compiler_params=_params(("arbitrary", "arbitrary")),
        name="lru_layer",
    )(x, g, w_in, cw, cb, wg, br, bi, lam)


def _pair_gate_weights(w_r, w_i):
    n, bd, _ = w_r.shape
    z = jnp.zeros((n // 2, bd, bd), w_r.dtype)

    def blockdiag(w):
        w = w.reshape(n // 2, 2, bd, bd)
        top = jnp.concatenate([w[:, 0], z], axis=2)
        bot = jnp.concatenate([z, w[:, 1]], axis=2)
        return jnp.concatenate([top, bot], axis=1)

    return jnp.concatenate([blockdiag(w_r), blockdiag(w_i)], axis=2)


def _mix_out_kernel(tok_ref, qm_ref, mk_ref, mv_ref, wo_ref, h_ref, g_ref, o_ref, *, tok_width, rc):
    scale = LOG2_E * HEAD_DIM ** -0.5
    for r in range(tok_ref.shape[0] // rc):
        rw = pl.ds(r * rc, rc)
        y = _dot(tok_ref[rw, :], wo_ref[0:tok_width, :])
        heads = []
        for hh in range(MEM_HEADS):
            sl = slice(hh * HEAD_DIM, (hh + 1) * HEAD_DIM)
            q = (qm_ref[rw, sl] * scale).astype(BF16)
            s = _dot_nt(q, mk_ref[0, 0, :, sl])
            p = jnp.exp2(s - jnp.max(s, axis=-1, keepdims=True))
            pv = _dot(p.astype(BF16), mv_ref[0, 0, :, 2 * hh * HEAD_DIM:(2 * hh + 2) * HEAD_DIM])
            heads.append((pv[:, :HEAD_DIM] / pv[:, HEAD_DIM:]).astype(BF16))
        y = y + _dot(jnp.concatenate(heads, axis=1), wo_ref[tok_width:, :])
        o_ref[rw, :] = h_ref[rw, :] + _rms(y, g_ref[...])


def _mix_out(tok, qm, mk, mv, layer, w_o, h, g_post, seq, tm):
    t, tok_width = tok.shape
    d = h.shape[1]
    ml, mw = mk.shape[2], mk.shape[3]
    bpb = seq // tm
    kern = functools.partial(_mix_out_kernel, tok_width=tok_width, rc=tm)
    return pl.pallas_call(
        kern,
        grid=(t // tm,),
        in_specs=[
            pl.BlockSpec((tm, tok_width), lambda i: (i, 0)),
            pl.BlockSpec((tm, mw), lambda i: (i, 0)),
            pl.BlockSpec((1, 1, ml, mw), lambda i: (layer, i // bpb, 0, 0)),
            pl.BlockSpec((1, 1, ml, 2 * mw), lambda i: (layer, i // bpb, 0, 0)),
            pl.BlockSpec((None, tok_width + mw, d), lambda i: (layer, 0, 0), pipeline_mode=pl.Buffered(1)),
            pl.BlockSpec((tm, d), lambda i: (i, 0)),
            pl.BlockSpec((1, d), lambda i: (0, 0)),
        ],
        out_specs=pl.BlockSpec((tm, d), lambda i: (i, 0)),
        out_shape=jax.ShapeDtypeStruct((t, d), F32),
        compiler_params=_params(("arbitrary",)),
        name="mix_out",
    )(tok, qm, mk, mv, w_o, h, g_post.reshape(1, d))


def _mlp_kernel(h_ref, gpre_ref, w1_ref, w2_ref, gpost_ref, o_ref, hn_ref, *, rc):
    j = pl.program_id(1)
    last = pl.num_programs(1) - 1
    tm = h_ref.shape[0]

    def ffn(hn, w):
        f = _dot(hn, w[0])
        return _dot(jnp.square(jnp.maximum(f, 0.0)).astype(BF16), w[1])

    def w2_block():
        return w1_ref[...].astype(BF16), w2_ref[...].astype(BF16)

    @pl.when(j == 0)
    def _():
        w2 = w2_block()
        for r in range(tm // rc):
            rows = pl.ds(r * rc, rc)
            hn = _rms(h_ref[rows, :], gpre_ref[...]).astype(BF16)
            hn_ref[rows, :] = hn
            o_ref[rows, :] = ffn(hn, w2)

    @pl.when(jnp.logical_and(j > 0, j < last))
    def _():
        o_ref[...] += ffn(hn_ref[...], w2_block())

    @pl.when(j == last)
    def _():
        w2 = w2_block()
        for r in range(tm // rc):
            rows = pl.ds(r * rc, rc)
            acc = o_ref[rows, :] + ffn(hn_ref[rows, :], w2)
            o_ref[rows, :] = h_ref[rows, :] + _rms(acc, gpost_ref[...])


def _mlp(h, g_pre, w1, w2, layer, g_post, tm, tf):
    t, d = h.shape
    dff = w1.shape[2]
    assert dff // tf >= 2
    return pl.pallas_call(
        functools.partial(_mlp_kernel, rc=512),
        grid=(t // tm, dff // tf),
        in_specs=[
            pl.BlockSpec((tm, d), lambda i, j: (i, 0)),
            pl.BlockSpec((1, d), lambda i, j: (0, 0)),
            pl.BlockSpec((None, d, tf), lambda i, j: (layer, 0, j)),
            pl.BlockSpec((None, tf, d), lambda i, j: (layer, j, 0)),
            pl.BlockSpec((1, d), lambda i, j: (0, 0)),
        ],
        out_specs=pl.BlockSpec((tm, d), lambda i, j: (i, 0)),
        out_shape=jax.ShapeDtypeStruct((t, d), F32),
        scratch_shapes=[pltpu.VMEM((tm, d), BF16)],
        compiler_params=_params(("arbitrary", "arbitrary"), vmem=MLP_VMEM_LIMIT),
        name="mlp",
    )(h, g_pre.reshape(1, d), w1, w2, g_post.reshape(1, d))


def _swap_halves(x, lane):
    return jnp.where(lane % ROPE_DIM < ROPE_DIM // 2, pltpu.roll(x, 128 - ROPE_DIM // 2, 1),
                     pltpu.roll(x, ROPE_DIM // 2, 1))


def _proj_b_kernel(h_ref, pos_ref, freq_ref, gkv_ref, gmix_ref, wdown_ref, glat_ref, wupk_ref, wupv_ref,
                   win_ref, gqa_ref, wqn_ref, wqp_ref,
                   q_ref, k_ref, v_ref, qm_ref, *, n_heads, kv_rank, q_rank, rc):
    for r in range(h_ref.shape[0] // rc):
        _proj_b_rows(pl.ds(r * rc, rc), rc, h_ref, pos_ref, freq_ref, gkv_ref, gmix_ref, wdown_ref, glat_ref,
                     wupk_ref, wupv_ref, win_ref, gqa_ref, wqn_ref, wqp_ref, q_ref, k_ref, v_ref, qm_ref,
                     n_heads, kv_rank, q_rank)


def _proj_b_rows(rw, tm, h_ref, pos_ref, freq_ref, gkv_ref, gmix_ref, wdown_ref, glat_ref, wupk_ref, wupv_ref,
                 win_ref, gqa_ref, wqn_ref, wqp_ref, q_ref, k_ref, v_ref, qm_ref, n_heads, kv_rank, q_rank):
    h = h_ref[rw, :]
    xn = h * lax.rsqrt(jnp.mean(h * h, axis=-1, keepdims=True) + EPS)

    lane = lax.broadcasted_iota(jnp.int32, (tm, 128), 1)
    ang = pos_ref[rw, :].astype(F32) * freq_ref[...]
    cos = jnp.cos(ang)
    sin = jnp.where(lane % ROPE_DIM < ROPE_DIM // 2, -jnp.sin(ang), jnp.sin(ang))
    low = lane < ROPE_DIM

    def rope(x):
        return x * cos + _swap_halves(x, lane) * sin

    ckv = _dot((xn * gkv_ref[...]).astype(BF16), wdown_ref[...])
    latent = _rms(ckv[:, :kv_rank], glat_ref[...]).astype(BF16)
    kpe_raw = jnp.concatenate([ckv[:, kv_rank:], jnp.zeros((tm, 128 - ROPE_DIM), F32)], axis=1)
    kpe = jnp.where(low, rope(kpe_raw), 0.0).astype(BF16)
    k_nope = _dot(latent, wupk_ref[...])
    v = _dot(latent, wupv_ref[...])
    ones = jnp.ones((tm, HEAD_DIM), BF16)
    for hh in range(n_heads):
        sl = slice(hh * HEAD_DIM, (hh + 1) * HEAD_DIM)
        k_ref[rw, hh * QK_PAD:hh * QK_PAD + HEAD_DIM] = k_nope[:, sl].astype(BF16)
        k_ref[rw, hh * QK_PAD + HEAD_DIM:(hh + 1) * QK_PAD] = kpe
        v_ref[rw, 2 * hh * HEAD_DIM:(2 * hh + 1) * HEAD_DIM] = v[:, sl].astype(BF16)
        v_ref[rw, (2 * hh + 1) * HEAD_DIM:(2 * hh + 2) * HEAD_DIM] = ones

    proj = _dot((xn * gmix_ref[...]).astype(BF16), win_ref[...])
    qm_ref[rw, :] = proj[:, q_rank:]
    cq = _rms(proj[:, :q_rank], gqa_ref[...]).astype(BF16)
    scale = LOG2_E * QK_HEAD_DIM ** -0.5
    q_nope = _dot(cq, wqn_ref[...]) * scale
    q_pe = _dot(cq, wqp_ref[...]) * scale
    for hp in range(n_heads // 2):
        pe = rope(q_pe[:, hp * 128:(hp + 1) * 128])
        for e in range(2):
            hh = 2 * hp + e
            q_ref[rw, hh * QK_PAD:hh * QK_PAD + HEAD_DIM] = q_nope[:, hh * HEAD_DIM:(hh + 1) * HEAD_DIM].astype(BF16)
            pe_h = pe if e == 0 else pltpu.roll(pe, ROPE_DIM, 1)
            q_ref[rw, hh * QK_PAD + HEAD_DIM:(hh + 1) * QK_PAD] = jnp.where(low, pe_h, 0.0).astype(BF16)


def _proj_b(h, pos, freq, g_kv, g_mix, w_down, g_lat, w_upk, w_upv, w_in, g_qa, w_qn, w_qp, tm):
    t, d = h.shape
    kv_rank = w_upk.shape[0]
    q_rank = w_qn.shape[0]
    n_heads = w_upk.shape[1] // HEAD_DIM
    mw = w_in.shape[1] - q_rank
    g_kv, g_mix, g_lat, g_qa = (g_kv.reshape(1, -1), g_mix.reshape(1, -1), g_lat.reshape(1, -1),
                                g_qa.reshape(1, -1))
    consts = (freq, g_kv, g_mix, w_down, g_lat, w_upk, w_upv, w_in, g_qa, w_qn, w_qp)
    kern = functools.partial(_proj_b_kernel, n_heads=n_heads, kv_rank=kv_rank, q_rank=q_rank, rc=256)
    return pl.pallas_call(
        kern,
        grid=(t // tm,),
        in_specs=[
            pl.BlockSpec((tm, d), lambda i: (i, 0)),
            pl.BlockSpec((tm, 1), lambda i: (i, 0)),
        ] + [_const_spec(a) for a in consts],
        out_specs=[
            pl.BlockSpec((tm, n_heads * QK_PAD), lambda i: (i, 0)),
            pl.BlockSpec((tm, n_heads * QK_PAD), lambda i: (i, 0)),
            pl.BlockSpec((tm, 2 * n_heads * HEAD_DIM), lambda i: (i, 0)),
            pl.BlockSpec((tm, mw), lambda i: (i, 0)),
        ],
        out_shape=[
            jax.ShapeDtypeStruct((t, n_heads * QK_PAD), BF16),
            jax.ShapeDtypeStruct((t, n_heads * QK_PAD), BF16),
            jax.ShapeDtypeStruct((t, 2 * n_heads * HEAD_DIM), BF16),
            jax.ShapeDtypeStruct((t, mw), F32),
        ],
        compiler_params=_params(("arbitrary",)),
        name="proj_b",
    )(h, pos, *consts)


def _mla_kernel(q_ref, k_ref, v_ref, o_ref, sa_ref, sb_ref, m_ref, acc_ref, *, tq, nq, hps):
    m_ref[...] = jnp.full_like(m_ref, -1e30)
    acc_ref[...] = jnp.zeros_like(acc_ref)

    def rows(blk):
        return pl.ds(pl.multiple_of(blk * tq, tq), tq)

    def scores(hd, qi, kb, s_ref):
        qk = slice(hd * QK_PAD, (hd + 1) * QK_PAD)
        s_ref[hd] = _dot_nt(q_ref[rows(qi), qk], k_ref[rows(kb), qk])

    def consume(hd, qi, kb, s_ref, visible):
        diagonal = visible is not None
        s = s_ref[hd]
        if diagonal:
            s = jnp.where(visible, s, -1e30)
        m_old = m_ref[hd]
        m_new = jnp.maximum(m_old, jnp.max(s, axis=-1, keepdims=True))
        alpha = jnp.exp2(m_old - m_new)
        p = jnp.exp2(s - jnp.concatenate([m_new] * (tq // 128), axis=1))
        pv = _dot(p.astype(BF16), v_ref[rows(kb), 2 * hd * HEAD_DIM:2 * (hd + 1) * HEAD_DIM])
        acc = jnp.concatenate([alpha, alpha], axis=1) * acc_ref[hd] + pv
        if diagonal:
            o_ref[rows(qi), hd * HEAD_DIM:(hd + 1) * HEAD_DIM] = (acc[:, :HEAD_DIM] / acc[:, HEAD_DIM:]).astype(BF16)
            m_ref[hd] = jnp.full((tq, 128), -1e30, F32)
            acc_ref[hd] = jnp.zeros((tq, 2 * HEAD_DIM), F32)
        else:
            acc_ref[hd] = acc
            m_ref[hd] = m_new

    def following(qi, kb):
        is_diag = kb == qi
        return jnp.where(is_diag, qi + 1, qi), jnp.where(is_diag, 0, kb + 1)

    for hd in range(hps):
        scores(hd, 0, 0, sa_ref)

    def pair(_, c):
        q0, k0 = c
        q1, k1 = following(q0, k0)
        q2, k2 = following(q1, k1)
        q2c = jnp.minimum(q2, nq - 1)
        d0, d1 = k0 == q0, k1 == q1

        for v0 in (False, True):
            for v1 in (False, True):
                @pl.when(jnp.logical_and(d0 == v0, d1 == v1))
                def _(v0=v0, v1=v1):
                    visible = None
                    if v0 or v1:
                        qc = lax.broadcasted_iota(jnp.int32, (tq, 1), 0) // CHUNK
                        kc = lax.broadcasted_iota(jnp.int32, (1, tq), 1) // CHUNK
                        visible = kc <= qc
                    for hd in range(hps):
                        scores(hd, q1, k1, sb_ref)
                    for hd in range(hps):
                        consume(hd, q0, k0, sa_ref, visible if v0 else None)
                    for hd in range(hps):
                        scores(hd, q2c, k2, sa_ref)
                    for hd in range(hps):
                        consume(hd, q1, k1, sb_ref, visible if v1 else None)

        return q2, k2

    n_items = nq * (nq + 1) // 2
    lax.fori_loop(0, n_items // 2, pair, (jnp.int32(0), jnp.int32(0)))


def _mla(q, k, v, nb, seq, n_heads, tq, hps):
    t = q.shape[0]
    nq = seq // tq
    assert (nq * (nq + 1) // 2) % 2 == 0
    assert n_heads % hps == 0
    kern = functools.partial(_mla_kernel, tq=tq, nq=nq, hps=hps)
    return pl.pallas_call(
        kern,
        grid=(nb, n_heads // hps),
        in_specs=[
            pl.BlockSpec((seq, hps * QK_PAD), lambda b, h: (b, h)),
            pl.BlockSpec((seq, hps * QK_PAD), lambda b, h: (b, h)),
            pl.BlockSpec((seq, hps * 2 * HEAD_DIM), lambda b, h: (b, h)),
        ],
        out_specs=pl.BlockSpec((seq, hps * HEAD_DIM), lambda b, h: (b, h)),
        out_shape=jax.ShapeDtypeStruct((t, n_heads * HEAD_DIM), BF16),
        scratch_shapes=[
            pltpu.VMEM((hps, tq, tq), F32),
            pltpu.VMEM((hps, tq, tq), F32),
            pltpu.VMEM((hps, tq, 128), F32),
            pltpu.VMEM((hps, tq, 2 * HEAD_DIM), F32),
        ],
        compiler_params=_params(("arbitrary", "arbitrary")),
        name="mla",
    )(q, k, v)


def kernel(x, mem, positions, g_mix_pre, g_mix_post, g_mlp_pre, g_mlp_post, g_mem, w_mem_k, w_mem_v, w_o, w_ff1, w_ff2, a_w_in, a_conv_w, a_conv_b, a_w_rgate, a_b_rgate, a_w_igate, a_b_igate, a_lambda, b_w_in, b_g_qa, b_w_qb, kv_g_in, kv_w_down, kv_g_latent, kv_w_up):
    nb, seq, d = x.shape
    t = nb * seq
    depth = g_mix_pre.shape[0]
    n_a = a_w_in.shape[0]
    lru_width = a_conv_w.shape[2]
    n_heads = lru_width // HEAD_DIM
    kv_rank = kv_g_latent.shape[0]
    q_rank = b_g_qa.shape[1]
    assert kv_w_down.shape[1] == kv_rank + ROPE_DIM
    assert seq % 512 == 0 and t % 1024 == 0 and d % 128 == 0
    assert depth - n_a == 1, "the shared K/V projection is fused into the single MLA layer's projections"

    bf = lambda w: w.astype(BF16)
    mk, mv = _mem_kv(mem, g_mem, bf(w_mem_k), bf(w_mem_v))
    w_o16 = bf(w_o)

    half = ROPE_DIM // 2
    inv_freq = ROPE_THETA ** (-jnp.arange(half, dtype=F32) / half)
    freq = jnp.tile(inv_freq, 128 // half).reshape(1, 128)
    pos = positions.reshape(t, 1)

    w_up = bf(kv_w_up).reshape(kv_rank, n_heads, 2, HEAD_DIM)
    w_upk = w_up[:, :, 0, :].reshape(kv_rank, n_heads * HEAD_DIM)
    w_upv = w_up[:, :, 1, :].reshape(kv_rank, n_heads * HEAD_DIM)

    h = x.reshape(t, d)
    for layer in range(depth):
        if layer < n_a:
            la = layer
            wg = _pair_gate_weights(bf(a_w_rgate[la]), bf(a_w_igate[la]))
            tok, qm = _lru_layer(h, g_mix_pre[layer], bf(a_w_in[la]), nb, seq, a_conv_w[la], a_conv_b[la], wg,
                                 a_b_rgate[la], a_b_igate[la], a_lambda[la], ts=512)
        else:
            lb = layer - n_a
            w_qb = bf(b_w_qb[lb]).reshape(q_rank, n_heads, QK_HEAD_DIM)
            w_qn = w_qb[:, :, :HEAD_DIM].reshape(q_rank, n_heads * HEAD_DIM)
            w_qp = w_qb[:, :, HEAD_DIM:].reshape(q_rank, n_heads * ROPE_DIM)
            q, k, v, qm = _proj_b(h, pos, freq, kv_g_in, g_mix_pre[layer], bf(kv_w_down), kv_g_latent,
                                  w_upk, w_upv, bf(b_w_in[lb]), b_g_qa[lb], w_qn, w_qp, tm=512)
            tok = _mla(q, k, v, nb, seq, n_heads, tq=512, hps=2)
        h = _mix_out(tok, qm, mk, mv, layer, w_o16, h, g_mix_post[layer], seq, tm=512)
        h = _mlp(h, g_mlp_pre[layer], w_ff1, w_ff2, layer, g_mlp_post[layer], tm=1024, tf=512)
    return h.reshape(nb, seq, d)
```

```python
import functools

import jax
import jax.numpy as jnp
from jax import lax
from jax.experimental import pallas as pl
from jax.experimental.pallas import tpu as pltpu

EPS = 1e-6
LRU_C = 8.0
CHUNK = 64
ROPE_THETA = 10000.0
LOG2_E = 1.4426950408889634
F32_TINY = 1.1754943508222875e-38
MEM_HEADS = 4
HEAD_DIM = 128
ROPE_DIM = 64
QK_HEAD_DIM = HEAD_DIM + ROPE_DIM
QK_PAD = 256
SUBLANES = 8
VMEM_LIMIT = 56 * 1024 * 1024
MLP_VMEM_LIMIT = 61 * 1024 * 1024

BF16 = jnp.bfloat16
F32 = jnp.float32


def _params(sem, vmem=VMEM_LIMIT):
    return pltpu.CompilerParams(dimension_semantics=sem, vmem_limit_bytes=vmem)


def _const_spec(a):
    return pl.BlockSpec(a.shape, lambda *_: (0,) * a.ndim, pipeline_mode=pl.Buffered(1))


def _rms(x, g):
    return x * lax.rsqrt(jnp.mean(x * x, axis=-1, keepdims=True) + EPS) * g


def _dot(a, b):
    return jnp.dot(a, b, preferred_element_type=F32)


def _zero_from(x):
    bits = lax.bitcast_convert_type(x, jnp.int32)
    return lax.shift_right_logical(lax.shift_right_logical(bits, 16), 16).astype(F32)


def _dot_nt(a, b):
    return lax.dot_general(a, b, (((1,), (1,)), ((), ())), preferred_element_type=F32)


def _mem_kv_kernel(mem_ref, g_ref, wk_ref, wv_ref, mk_ref, mv_ref):
    mn = _rms(mem_ref[0], g_ref[0]).astype(BF16)
    mk_ref[0, 0] = _dot(mn, wk_ref[0]).astype(BF16)
    mv = _dot(mn, wv_ref[0]).astype(BF16)
    ones = jnp.ones((mv.shape[0], HEAD_DIM), BF16)
    for hh in range(MEM_HEADS):
        mv_ref[0, 0, :, 2 * hh * HEAD_DIM:(2 * hh + 1) * HEAD_DIM] = mv[:, hh * HEAD_DIM:(hh + 1) * HEAD_DIM]
        mv_ref[0, 0, :, (2 * hh + 1) * HEAD_DIM:(2 * hh + 2) * HEAD_DIM] = ones


def _mem_kv(mem, g_mem, w_k, w_v):
    nb, ml, d = mem.shape
    nl, _, mw = w_k.shape
    assert mw == MEM_HEADS * HEAD_DIM
    return pl.pallas_call(
        _mem_kv_kernel,
        grid=(nl, nb),
        in_specs=[
            pl.BlockSpec((1, ml, d), lambda l, b: (b, 0, 0)),
            pl.BlockSpec((1, 1, d), lambda l, b: (l, 0, 0)),
            pl.BlockSpec((1, d, mw), lambda l, b: (l, 0, 0)),
            pl.BlockSpec((1, d, mw), lambda l, b: (l, 0, 0)),
        ],
        out_specs=[
            pl.BlockSpec((1, 1, ml, mw), lambda l, b: (l, b, 0, 0)),
            pl.BlockSpec((1, 1, ml, 2 * mw), lambda l, b: (l, b, 0, 0)),
        ],
        out_shape=[jax.ShapeDtypeStruct((nl, nb, ml, mw), BF16),
                   jax.ShapeDtypeStruct((nl, nb, ml, 2 * mw), BF16)],
        compiler_params=_params(("arbitrary", "arbitrary")),
        name="mem_kv",
    )(mem, g_mem.reshape(nl, 1, d), w_k, w_v)


def _lru_layer_kernel(x_ref, g_ref, w_ref, cw_ref, cb_ref, wg_ref, br_ref, bi_ref, lam_ref,
                      tok_ref, qm_ref, hn_ref, tail_ref, carry_ref, *, ts, conv_width):
    c = cw_ref.shape[1]
    pw = 2 * HEAD_DIM

    @pl.when(pl.program_id(1) == 0)
    def _():
        tail_ref[...] = jnp.zeros_like(tail_ref)
        carry_ref[...] = jnp.zeros_like(carry_ref)

    hn_ref[...] = _rms(x_ref[...], g_ref[...]).astype(BF16)
    neg_c_sp = -LRU_C * jax.nn.softplus(-lam_ref[...])
    row = lax.broadcasted_iota(jnp.int32, (SUBLANES, pw), 0)

    def sigmoid(z):
        return 0.5 * jnp.tanh(0.5 * z) + 0.5

    def x_branch(p):
        return _dot(hn_ref[...], w_ref[:, p * pw:(p + 1) * pw])

    xb_next = x_branch(0)
    for p in range(c // pw):
        sl = slice(p * pw, (p + 1) * pw)
        xb = xb_next
        if p + 1 < c // pw:
            xb_next = x_branch(p + 1)
        tail = tail_ref[:, sl]
        xc = cb_ref[:, sl] + cw_ref[conv_width - 1:conv_width, sl] * xb
        for j in range(1, conv_width):
            shifted = pltpu.roll(xb, j, 0)
            top = jnp.where(row < j, pltpu.roll(tail, j, 0), shifted[:SUBLANES, :])
            shifted = jnp.concatenate([top, shifted[SUBLANES:, :]], axis=0)
            xc = xc + cw_ref[conv_width - 1 - j:conv_width - j, sl] * shifted
        tail_ref[:, sl] = xb[ts - SUBLANES:, :]
        ri = _dot(xc.astype(BF16), wg_ref[p])
        r = sigmoid(ri[:, :pw] + br_ref[:, sl])
        i = sigmoid(ri[:, pw:] + bi_ref[:, sl])
        log_a = r * neg_c_sp[:, sl]
        a_all = jnp.exp(log_a)
        y = jnp.tanh(-log_a) * (1.0 + a_all * a_all)
        b_all = (y * lax.rsqrt(jnp.maximum(y, F32_TINY))) * (i * xc)

        carry = carry_ref[:, sl]
        hs = []
        for g in range(ts // SUBLANES):
            a = a_all[g * SUBLANES:(g + 1) * SUBLANES, :]
            b = b_all[g * SUBLANES:(g + 1) * SUBLANES, :]
            for d in (1, 2, 4):
                keep = row >= d
                a_sh = pltpu.roll(a, d, 0)
                b_sh = pltpu.roll(b, d, 0)
                b = jnp.where(keep, a * b_sh + b, b)
                a = jnp.where(keep, a * a_sh, a)
            h = b + a * carry
            hs.append(h)
            carry = jnp.broadcast_to(h[SUBLANES - 1:SUBLANES, :], (SUBLANES, pw))
        carry_ref[:, sl] = carry

        gate = _dot(hn_ref[...], w_ref[:, c + p * pw:c + (p + 1) * pw])
        tok_ref[:, sl] = (jnp.concatenate(hs, axis=0) * jax.nn.gelu(gate)).astype(BF16)
    qm_ref[...] = _dot(hn_ref[...], w_ref[:, 2 * c:])


def _lru_layer(x, g, w_in, nb, seq, cw, cb, wg, br, bi, lam, ts):
    t, d = x.shape
    c = cw.shape[1]
    mw = w_in.shape[1] - 2 * c
    spb = seq // ts
    g, cb, br, bi, lam = (v.reshape(1, -1) for v in (g, cb, br, bi, lam))
    kern = functools.partial(_lru_layer_kernel, ts=ts, conv_width=cw.shape[0])
    return pl.pallas_call(
        kern,
        grid=(nb, spb),
        in_specs=[pl.BlockSpec((ts, d), lambda b, s: (b * spb + s, 0))]
                 + [_const_spec(v) for v in (g, w_in, cw, cb, wg, br, bi, lam)],
        out_specs=[
            pl.BlockSpec((ts, c), lambda b, s: (b * spb + s, 0)),
            pl.BlockSpec((ts, mw), lambda b, s: (b * spb + s, 0)),
        ],
        out_shape=[jax.ShapeDtypeStruct((t, c), BF16), jax.ShapeDtypeStruct((t, mw), F32)],
        scratch_shapes=[
            pltpu.VMEM((ts, d), BF16),
            pltpu.VMEM((SUBLANES, c), F32),
            pltpu.VMEM((SUBLANES, c), F32),
        ],
        compiler_params=_params(("arbitrary", "arbitrary")),
        name="lru_layer",
    )(x, g, w_in, cw, cb, wg, br, bi, lam)


def _pair_gate_weights(w_r, w_i):
    n, bd, _ = w_r.shape
    z = jnp.zeros((n // 2, bd, bd), w_r.dtype)

    def blockdiag(w):
        w = w.reshape(n // 2, 2, bd, bd)
        top = jnp.concatenate([w[:, 0], z], axis=2)
        bot = jnp.concatenate([z, w[:, 1]], axis=2)
        return jnp.concatenate([top, bot], axis=1)

    return jnp.concatenate([blockdiag(w_r), blockdiag(w_i)], axis=2)


def _mix_out_kernel(tok_ref, qm_ref, mk_ref, mv_ref, wo_ref, h_ref, g_ref, o_ref, *, tok_width):
    scale = LOG2_E * HEAD_DIM ** -0.5
    y = _dot(tok_ref[...], wo_ref[0:tok_width, :])
    heads = []
    for hh in range(MEM_HEADS):
        sl = slice(hh * HEAD_DIM, (hh + 1) * HEAD_DIM)
        q = (qm_ref[:, sl] * scale).astype(BF16)
        s = _dot_nt(q, mk_ref[0, 0, :, sl])
        p = jnp.exp2(s - jnp.max(s, axis=-1, keepdims=True))
        pv = _dot(p.astype(BF16), mv_ref[0, 0, :, 2 * hh * HEAD_DIM:(2 * hh + 2) * HEAD_DIM])
        heads.append((pv[:, :HEAD_DIM] / pv[:, HEAD_DIM:]).astype(BF16))
    y = y + _dot(jnp.concatenate(heads, axis=1), wo_ref[tok_width:, :])
    o_ref[...] = h_ref[...] + _rms(y, g_ref[...])


def _mix_out(tok, qm, mk, mv, layer, w_o, h, g_post, seq, tm):
    t, tok_width = tok.shape
    d = h.shape[1]
    ml, mw = mk.shape[2], mk.shape[3]
    bpb = seq // tm
    kern = functools.partial(_mix_out_kernel, tok_width=tok_width)
    return pl.pallas_call(
        kern,
        grid=(t // tm,),
        in_specs=[
            pl.BlockSpec((tm, tok_width), lambda i: (i, 0)),
            pl.BlockSpec((tm, mw), lambda i: (i, 0)),
            pl.BlockSpec((1, 1, ml, mw), lambda i: (layer, i // bpb, 0, 0)),
            pl.BlockSpec((1, 1, ml, 2 * mw), lambda i: (layer, i // bpb, 0, 0)),
            pl.BlockSpec((None, tok_width + mw, d), lambda i: (layer, 0, 0), pipeline_mode=pl.Buffered(1)),
            pl.BlockSpec((tm, d), lambda i: (i, 0)),
            pl.BlockSpec((1, d), lambda i: (0, 0)),
        ],
        out_specs=pl.BlockSpec((tm, d), lambda i: (i, 0)),
        out_shape=jax.ShapeDtypeStruct((t, d), F32),
        compiler_params=_params(("arbitrary",)),
        name="mix_out",
    )(tok, qm, mk, mv, w_o, h, g_post.reshape(1, d))


def _mlp_kernel(h_ref, gpre_ref, w1_ref, w2_ref, gpost_ref, o_ref, hn_ref, *, rc):
    j = pl.program_id(1)
    last = pl.num_programs(1) - 1
    tm = h_ref.shape[0]

    def ffn(hn, w):
        f = _dot(hn, w[0])
        return _dot(jnp.square(jnp.maximum(f, 0.0)).astype(BF16), w[1])

    def w2_block():
        return w1_ref[...].astype(BF16), w2_ref[...].astype(BF16)

    @pl.when(j == 0)
    def _():
        w2 = w2_block()
        for r in range(tm // rc):
            rows = pl.ds(r * rc, rc)
            hn = _rms(h_ref[rows, :], gpre_ref[...]).astype(BF16)
            hn_ref[rows, :] = hn
            o_ref[rows, :] = ffn(hn, w2)

    @pl.when(jnp.logical_and(j > 0, j < last))
    def _():
        o_ref[...] += ffn(hn_ref[...], w2_block())

    @pl.when(j == last)
    def _():
        w2 = w2_block()
        for r in range(tm // rc):
            rows = pl.ds(r * rc, rc)
            acc = o_ref[rows, :] + ffn(hn_ref[rows, :], w2)
            o_ref[rows, :] = h_ref[rows, :] + _rms(acc, gpost_ref[...])


def _mlp(h, g_pre, w1, w2, layer, g_post, tm, tf):
    t, d = h.shape
    dff = w1.shape[2]
    assert dff // tf >= 2
    return pl.pallas_call(
        functools.partial(_mlp_kernel, rc=512),
        grid=(t // tm, dff // tf),
        in_specs=[
            pl.BlockSpec((tm, d), lambda i, j: (i, 0)),
            pl.BlockSpec((1, d), lambda i, j: (0, 0)),
            pl.BlockSpec((None, d, tf), lambda i, j: (layer, 0, j)),
            pl.BlockSpec((None, tf, d), lambda i, j: (layer, j, 0)),
            pl.BlockSpec((1, d), lambda i, j: (0, 0)),
        ],
        out_specs=pl.BlockSpec((tm, d), lambda i, j: (i, 0)),
        out_shape=jax.ShapeDtypeStruct((t, d), F32),
        scratch_shapes=[pltpu.VMEM((tm, d), BF16)],
        compiler_params=_params(("arbitrary", "arbitrary"), vmem=MLP_VMEM_LIMIT),
        name="mlp",
    )(h, g_pre.reshape(1, d), w1, w2, g_post.reshape(1, d))


def _swap_halves(x, lane):
    return jnp.where(lane % ROPE_DIM < ROPE_DIM // 2, pltpu.roll(x, 128 - ROPE_DIM // 2, 1),
                     pltpu.roll(x, ROPE_DIM // 2, 1))


def _proj_b_kernel(h_ref, pos_ref, freq_ref, gkv_ref, gmix_ref, wdown_ref, glat_ref, wupk_ref, wupv_ref,
                   win_ref, gqa_ref, wqn_ref, wqp_ref,
                   q_ref, k_ref, v_ref, qm_ref, *, n_heads, kv_rank, q_rank, rc):
    for r in range(h_ref.shape[0] // rc):
        _proj_b_rows(pl.ds(r * rc, rc), rc, h_ref, pos_ref, freq_ref, gkv_ref, gmix_ref, wdown_ref, glat_ref,
                     wupk_ref, wupv_ref, win_ref, gqa_ref, wqn_ref, wqp_ref, q_ref, k_ref, v_ref, qm_ref,
                     n_heads, kv_rank, q_rank)


def _proj_b_rows(rw, tm, h_ref, pos_ref, freq_ref, gkv_ref, gmix_ref, wdown_ref, glat_ref, wupk_ref, wupv_ref,
                 win_ref, gqa_ref, wqn_ref, wqp_ref, q_ref, k_ref, v_ref, qm_ref, n_heads, kv_rank, q_rank):
    h = h_ref[rw, :]
    xn = h * lax.rsqrt(jnp.mean(h * h, axis=-1, keepdims=True) + EPS)

    lane = lax.broadcasted_iota(jnp.int32, (tm, 128), 1)
    ang = pos_ref[rw, :].astype(F32) * freq_ref[...]
    cos = jnp.cos(ang)
    sin = jnp.where(lane % ROPE_DIM < ROPE_DIM // 2, -jnp.sin(ang), jnp.sin(ang))
    low = lane < ROPE_DIM

    def rope(x):
        return x * cos + _swap_halves(x, lane) * sin

    def kv_path(after):
        ckv = _dot((xn * gkv_ref[...]).astype(BF16), wdown_ref[...])
        latent = (_rms(ckv[:, :kv_rank], glat_ref[...])
                  + jnp.concatenate([after] * (kv_rank // 128), axis=1)).astype(BF16)
        kpe_raw = jnp.concatenate([ckv[:, kv_rank:], jnp.zeros((tm, 128 - ROPE_DIM), F32)], axis=1)
        kpe = jnp.where(low, rope(kpe_raw), 0.0).astype(BF16)
        k_nope = _dot(latent, wupk_ref[...])
        v = _dot(latent, wupv_ref[...])
        ones = jnp.ones((tm, HEAD_DIM), BF16)
        for hh in range(n_heads):
            sl = slice(hh * HEAD_DIM, (hh + 1) * HEAD_DIM)
            k_ref[rw, hh * QK_PAD:hh * QK_PAD + HEAD_DIM] = k_nope[:, sl].astype(BF16)
            k_ref[rw, hh * QK_PAD + HEAD_DIM:(hh + 1) * QK_PAD] = kpe
            v_ref[rw, 2 * hh * HEAD_DIM:(2 * hh + 1) * HEAD_DIM] = v[:, sl].astype(BF16)
            v_ref[rw, (2 * hh + 1) * HEAD_DIM:(2 * hh + 2) * HEAD_DIM] = ones

    def q_path():
        proj = _dot((xn * gmix_ref[...]).astype(BF16), win_ref[...])
        qm_ref[rw, :] = proj[:, q_rank:]
        cq = _rms(proj[:, :q_rank], gqa_ref[...]).astype(BF16)
        scale = LOG2_E * QK_HEAD_DIM ** -0.5
        q_nope = _dot(cq, wqn_ref[...]) * scale
        q_pe = _dot(cq, wqp_ref[...]) * scale
        for hp in range(n_heads // 2):
            pe = rope(q_pe[:, hp * 128:(hp + 1) * 128])
            for e in range(2):
                hh = 2 * hp + e
                q_ref[rw, hh * QK_PAD:hh * QK_PAD + HEAD_DIM] = q_nope[:, hh * HEAD_DIM:(hh + 1) * HEAD_DIM].astype(BF16)
                pe_h = pe if e == 0 else pltpu.roll(pe, ROPE_DIM, 1)
                q_ref[rw, hh * QK_PAD + HEAD_DIM:(hh + 1) * QK_PAD] = jnp.where(low, pe_h, 0.0).astype(BF16)
        return pe

    kv_path(_zero_from(cos + sin))
    q_path()


def _proj_b(h, pos, freq, g_kv, g_mix, w_down, g_lat, w_upk, w_upv, w_in, g_qa, w_qn, w_qp, tm):
    t, d = h.shape
    kv_rank = w_upk.shape[0]
    q_rank = w_qn.shape[0]
    n_heads = w_upk.shape[1] // HEAD_DIM
    mw = w_in.shape[1] - q_rank
    g_kv, g_mix, g_lat, g_qa = (g_kv.reshape(1, -1), g_mix.reshape(1, -1), g_lat.reshape(1, -1),
                                g_qa.reshape(1, -1))
    consts = (freq, g_kv, g_mix, w_down, g_lat, w_upk, w_upv, w_in, g_qa, w_qn, w_qp)
    kern = functools.partial(_proj_b_kernel, n_heads=n_heads, kv_rank=kv_rank, q_rank=q_rank, rc=256)
    return pl.pallas_call(
        kern,
        grid=(t // tm,),
        in_specs=[
            pl.BlockSpec((tm, d), lambda i: (i, 0)),
            pl.BlockSpec((tm, 1), lambda i: (i, 0)),
        ] + [_const_spec(a) for a in consts],
        out_specs=[
            pl.BlockSpec((tm, n_heads * QK_PAD), lambda i: (i, 0)),
            pl.BlockSpec((tm, n_heads * QK_PAD), lambda i: (i, 0)),
            pl.BlockSpec((tm, 2 * n_heads * HEAD_DIM), lambda i: (i, 0)),
            pl.BlockSpec((tm, mw), lambda i: (i, 0)),
        ],
        out_shape=[
            jax.ShapeDtypeStruct((t, n_heads * QK_PAD), BF16),
            jax.ShapeDtypeStruct((t, n_heads * QK_PAD), BF16),
            jax.ShapeDtypeStruct((t, 2 * n_heads * HEAD_DIM), BF16),
            jax.ShapeDtypeStruct((t, mw), F32),
        ],
        compiler_params=_params(("arbitrary",)),
        name="proj_b",
    )(h, pos, *consts)


def _mla_kernel(q_ref, k_ref, v_ref, o_ref, sa_ref, sb_ref, m_ref, acc_ref, *, tq, nq, hps):
    m_ref[...] = jnp.full_like(m_ref, -1e30)
    acc_ref[...] = jnp.zeros_like(acc_ref)

    def rows(blk):
        return pl.ds(pl.multiple_of(blk * tq, tq), tq)

    def scores(hd, qi, kb, s_ref):
        qk = slice(hd * QK_PAD, (hd + 1) * QK_PAD)
        s_ref[hd] = _dot_nt(q_ref[rows(qi), qk], k_ref[rows(kb), qk])

    def consume(hd, qi, kb, s_ref, visible):
        diagonal = visible is not None
        s = s_ref[hd]
        if diagonal:
            s = jnp.where(visible, s, -1e30)
        m_old = m_ref[hd]
        m_new = jnp.maximum(m_old, jnp.max(s, axis=-1, keepdims=True))
        alpha = jnp.exp2(m_old - m_new)
        p = jnp.exp2(s - jnp.concatenate([m_new] * (tq // 128), axis=1))
        pv = _dot(p.astype(BF16), v_ref[rows(kb), 2 * hd * HEAD_DIM:2 * (hd + 1) * HEAD_DIM])
        acc = jnp.concatenate([alpha, alpha], axis=1) * acc_ref[hd] + pv
        if diagonal:
            o_ref[rows(qi), hd * HEAD_DIM:(hd + 1) * HEAD_DIM] = (acc[:, :HEAD_DIM] / acc[:, HEAD_DIM:]).astype(BF16)
            m_ref[hd] = jnp.full((tq, 128), -1e30, F32)
            acc_ref[hd] = jnp.zeros((tq, 2 * HEAD_DIM), F32)
        else:
            acc_ref[hd] = acc
            m_ref[hd] = m_new

    def following(qi, kb):
        is_diag = kb == qi
        return jnp.where(is_diag, qi + 1, qi), jnp.where(is_diag, 0, kb + 1)

    for hd in range(hps):
        scores(hd, 0, 0, sa_ref)

    def pair(_, c):
        q0, k0 = c
        q1, k1 = following(q0, k0)
        q2, k2 = following(q1, k1)
        q2c = jnp.minimum(q2, nq - 1)
        d0, d1 = k0 == q0, k1 == q1

        for v0 in (False, True):
            for v1 in (False, True):
                @pl.when(jnp.logical_and(d0 == v0, d1 == v1))
                def _(v0=v0, v1=v1):
                    visible = None
                    if v0 or v1:
                        qc = lax.broadcasted_iota(jnp.int32, (tq, 1), 0) // CHUNK
                        kc = lax.broadcasted_iota(jnp.int32, (1, tq), 1) // CHUNK
                        visible = kc <= qc
                    for hd in range(hps):
                        scores(hd, q1, k1, sb_ref)
                    for hd in range(hps):
                        consume(hd, q0, k0, sa_ref, visible if v0 else None)
                    for hd in range(hps):
                        scores(hd, q2c, k2, sa_ref)
                    for hd in range(hps):
                        consume(hd, q1, k1, sb_ref, visible if v1 else None)

        return q2, k2

    n_items = nq * (nq + 1) // 2
    lax.fori_loop(0, n_items // 2, pair, (jnp.int32(0), jnp.int32(0)))


def _mla(q, k, v, nb, seq, n_heads, tq, hps):
    t = q.shape[0]
    nq = seq // tq
    assert (nq * (nq + 1) // 2) % 2 == 0
    assert n_heads % hps == 0
    kern = functools.partial(_mla_kernel, tq=tq, nq=nq, hps=hps)
    return pl.pallas_call(
        kern,
        grid=(nb, n_heads // hps),
        in_specs=[
            pl.BlockSpec((seq, hps * QK_PAD), lambda b, h: (b, h)),
            pl.BlockSpec((seq, hps * QK_PAD), lambda b, h: (b, h)),
            pl.BlockSpec((seq, hps * 2 * HEAD_DIM), lambda b, h: (b, h)),
        ],
        out_specs=pl.BlockSpec((seq, hps * HEAD_DIM), lambda b, h: (b, h)),
        out_shape=jax.ShapeDtypeStruct((t, n_heads * HEAD_DIM), BF16),
        scratch_shapes=[
            pltpu.VMEM((hps, tq, tq), F32),
            pltpu.VMEM((hps, tq, tq), F32),
            pltpu.VMEM((hps, tq, 128), F32),
            pltpu.VMEM((hps, tq, 2 * HEAD_DIM), F32),
        ],
        compiler_params=_params(("arbitrary", "arbitrary")),
        name="mla",
    )(q, k, v)


def kernel(x, mem, positions, g_mix_pre, g_mix_post, g_mlp_pre, g_mlp_post, g_mem, w_mem_k, w_mem_v, w_o, w_ff1, w_ff2, a_w_in, a_conv_w, a_conv_b, a_w_rgate, a_b_rgate, a_w_igate, a_b_igate, a_lambda, b_w_in, b_g_qa, b_w_qb, kv_g_in, kv_w_down, kv_g_latent, kv_w_up):
    nb, seq, d = x.shape
    t = nb * seq
    depth = g_mix_pre.shape[0]
    n_a = a_w_in.shape[0]
    lru_width = a_conv_w.shape[2]
    n_heads = lru_width // HEAD_DIM
    kv_rank = kv_g_latent.shape[0]
    q_rank = b_g_qa.shape[1]
    assert kv_w_down.shape[1] == kv_rank + ROPE_DIM
    assert seq % 512 == 0 and t % 1024 == 0 and d % 128 == 0
    assert depth - n_a == 1, "the shared K/V projection is fused into the single MLA layer's projections"

    bf = lambda w: w.astype(BF16)
    mk, mv = _mem_kv(mem, g_mem, bf(w_mem_k), bf(w_mem_v))
    w_o16 = bf(w_o)

    half = ROPE_DIM // 2
    inv_freq = ROPE_THETA ** (-jnp.arange(half, dtype=F32) / half)
    freq = jnp.tile(inv_freq, 128 // half).reshape(1, 128)
    pos = positions.reshape(t, 1)

    w_up = bf(kv_w_up).reshape(kv_rank, n_heads, 2, HEAD_DIM)
    w_upk = w_up[:, :, 0, :].reshape(kv_rank, n_heads * HEAD_DIM)
    w_upv = w_up[:, :, 1, :].reshape(kv_rank, n_heads * HEAD_DIM)

    h = x.reshape(t, d)
    for layer in range(depth):
        if layer < n_a:
            la = layer
            wg = _pair_gate_weights(bf(a_w_rgate[la]), bf(a_w_igate[la]))
            tok, qm = _lru_layer(h, g_mix_pre[layer], bf(a_w_in[la]), nb, seq, a_conv_w[la], a_conv_b[la], wg,
                                 a_b_rgate[la], a_b_igate[la], a_lambda[la], ts=512)
        else:
            lb = layer - n_a
            w_qb = bf(b_w_qb[lb]).reshape(q_rank, n_heads, QK_HEAD_DIM)
            w_qn = w_qb[:, :, :HEAD_DIM].reshape(q_rank, n_heads * HEAD_DIM)
            w_qp = w_qb[:, :, HEAD_DIM:].reshape(q_rank, n_heads * ROPE_DIM)
            q, k, v, qm = _proj_b(h, pos, freq, kv_g_in, g_mix_pre[layer], bf(kv_w_down), kv_g_latent,
                                  w_upk, w_upv, bf(b_w_in[lb]), b_g_qa[lb], w_qn, w_qp, tm=512)
            tok = _mla(q, k, v, nb, seq, n_heads, tq=512, hps=2)
        h = _mix_out(tok, qm, mk, mv, layer, w_o16, h, g_mix_post[layer], seq, tm=512)
        h = _mlp(h, g_mlp_pre[layer], w_ff1, w_ff2, layer, g_mlp_post[layer], tm=1024, tf=512)
    return h.reshape(nb, seq, d)
```

```python
import functools

import jax
import jax.numpy as jnp
from jax import lax
from jax.experimental import pallas as pl
from jax.experimental.pallas import tpu as pltpu

EPS = 1e-6
LRU_C = 8.0
CHUNK = 64
ROPE_THETA = 10000.0
LOG2_E = 1.4426950408889634
F32_TINY = 1.1754943508222875e-38
MEM_HEADS = 4
HEAD_DIM = 128
ROPE_DIM = 64
QK_HEAD_DIM = HEAD_DIM + ROPE_DIM
QK_PAD = 256
SUBLANES = 8
VMEM_LIMIT = 56 * 1024 * 1024
MLP_VMEM_LIMIT = 61 * 1024 * 1024

BF16 = jnp.bfloat16
F32 = jnp.float32


def _params(sem, vmem=VMEM_LIMIT):
    return pltpu.CompilerParams(dimension_semantics=sem, vmem_limit_bytes=vmem)


def _const_spec(a):
    return pl.BlockSpec(a.shape, lambda *_: (0,) * a.ndim, pipeline_mode=pl.Buffered(1))


def _rms(x, g):
    return x * lax.rsqrt(jnp.mean(x * x, axis=-1, keepdims=True) + EPS) * g


def _dot(a, b):
    return jnp.dot(a, b, preferred_element_type=F32)


def _zero_from(x):
    bits = lax.bitcast_convert_type(x, jnp.int32)
    return lax.shift_right_logical(lax.shift_right_logical(bits, 16), 16).astype(F32)


def _dot_nt(a, b):
    return lax.dot_general(a, b, (((1,), (1,)), ((), ())), preferred_element_type=F32)


def _mem_kv_kernel(mem_ref, g_ref, wk_ref, wv_ref, mk_ref, mv_ref):
    mn = _rms(mem_ref[0], g_ref[0]).astype(BF16)
    mk_ref[0, 0] = _dot(mn, wk_ref[0]).astype(BF16)
    mv = _dot(mn, wv_ref[0]).astype(BF16)
    ones = jnp.ones((mv.shape[0], HEAD_DIM), BF16)
    for hh in range(MEM_HEADS):
        mv_ref[0, 0, :, 2 * hh * HEAD_DIM:(2 * hh + 1) * HEAD_DIM] = mv[:, hh * HEAD_DIM:(hh + 1) * HEAD_DIM]
        mv_ref[0, 0, :, (2 * hh + 1) * HEAD_DIM:(2 * hh + 2) * HEAD_DIM] = ones


def _mem_kv(mem, g_mem, w_k, w_v):
    nb, ml, d = mem.shape
    nl, _, mw = w_k.shape
    assert mw == MEM_HEADS * HEAD_DIM
    return pl.pallas_call(
        _mem_kv_kernel,
        grid=(nl, nb),
        in_specs=[
            pl.BlockSpec((1, ml, d), lambda l, b: (b, 0, 0)),
            pl.BlockSpec((1, 1, d), lambda l, b: (l, 0, 0)),
            pl.BlockSpec((1, d, mw), lambda l, b: (l, 0, 0)),
            pl.BlockSpec((1, d, mw), lambda l, b: (l, 0, 0)),
        ],
        out_specs=[
            pl.BlockSpec((1, 1, ml, mw), lambda l, b: (l, b, 0, 0)),
            pl.BlockSpec((1, 1, ml, 2 * mw), lambda l, b: (l, b, 0, 0)),
        ],
        out_shape=[jax.ShapeDtypeStruct((nl, nb, ml, mw), BF16),
                   jax.ShapeDtypeStruct((nl, nb, ml, 2 * mw), BF16)],
        compiler_params=_params(("arbitrary", "arbitrary")),
        name="mem_kv",
    )(mem, g_mem.reshape(nl, 1, d), w_k, w_v)


def _lru_layer_kernel(x_ref, g_ref, w_ref, cw_ref, cb_ref, wg_ref, br_ref, bi_ref, lam_ref,
                      tok_ref, qm_ref, hn_ref, tail_ref, carry_ref, *, ts, conv_width):
    c = cw_ref.shape[1]
    pw = 2 * HEAD_DIM

    @pl.when(pl.program_id(1) == 0)
    def _():
        tail_ref[...] = jnp.zeros_like(tail_ref)
        carry_ref[...] = jnp.zeros_like(carry_ref)

    first = []
    for r in range(2):
        rw = pl.ds(r * (ts // 2), ts // 2)
        hn_ref[rw, :] = _rms(x_ref[rw, :], g_ref[...]).astype(BF16)
        first.append(_dot(hn_ref[rw, :], w_ref[:, 0:pw]))
    neg_c_sp = -LRU_C * jax.nn.softplus(-lam_ref[...])
    row = lax.broadcasted_iota(jnp.int32, (SUBLANES, pw), 0)

    def sigmoid(z):
        return 0.5 * jnp.tanh(0.5 * z) + 0.5

    def x_branch(p):
        return _dot(hn_ref[...], w_ref[:, p * pw:(p + 1) * pw])

    xb_next = jnp.concatenate(first, axis=0)
    for p in range(c // pw):
        sl = slice(p * pw, (p + 1) * pw)
        xb = xb_next
        if p + 1 < c // pw:
            xb_next = x_branch(p + 1)
        tail = tail_ref[:, sl]
        xc = cb_ref[:, sl] + cw_ref[conv_width - 1:conv_width, sl] * xb
        for j in range(1, conv_width):
            shifted = pltpu.roll(xb, j, 0)
            top = jnp.where(row < j, pltpu.roll(tail, j, 0), shifted[:SUBLANES, :])
            shifted = jnp.concatenate([top, shifted[SUBLANES:, :]], axis=0)
            xc = xc + cw_ref[conv_width - 1 - j:conv_width - j, sl] * shifted
        tail_ref[:, sl] = xb[ts - SUBLANES:, :]
        ri = _dot(xc.astype(BF16), wg_ref[p])
        r = sigmoid(ri[:, :pw] + br_ref[:, sl])
        i = sigmoid(ri[:, pw:] + bi_ref[:, sl])
        log_a = r * neg_c_sp[:, sl]
        a_all = jnp.exp(log_a)
        y = jnp.tanh(-log_a) * (1.0 + a_all * a_all)
        b_all = (y * lax.rsqrt(jnp.maximum(y, F32_TINY))) * (i * xc)

        carry = carry_ref[:, sl]
        hs = []
        for g in range(ts // SUBLANES):
            a = a_all[g * SUBLANES:(g + 1) * SUBLANES, :]
            b = b_all[g * SUBLANES:(g + 1) * SUBLANES, :]
            for d in (1, 2, 4):
                keep = row >= d
                a_sh = pltpu.roll(a, d, 0)
                b_sh = pltpu.roll(b, d, 0)
                b = jnp.where(keep, a * b_sh + b, b)
                a = jnp.where(keep, a * a_sh, a)
            h = b + a * carry
            hs.append(h)
            carry = jnp.broadcast_to(h[SUBLANES - 1:SUBLANES, :], (SUBLANES, pw))
        carry_ref[:, sl] = carry

        gate = _dot(hn_ref[...], w_ref[:, c + p * pw:c + (p + 1) * pw])
        tok_ref[:, sl] = (jnp.concatenate(hs, axis=0) * jax.nn.gelu(gate)).astype(BF16)
    qm_ref[...] = _dot(hn_ref[...], w_ref[:, 2 * c:])


def _lru_layer(x, g, w_in, nb, seq, cw, cb, wg, br, bi, lam, ts):
    t, d = x.shape
    c = cw.shape[1]
    mw = w_in.shape[1] - 2 * c
    spb = seq // ts
    g, cb, br, bi, lam = (v.reshape(1, -1) for v in (g, cb, br, bi, lam))
    kern = functools.partial(_lru_layer_kernel, ts=ts, conv_width=cw.shape[0])
    return pl.pallas_call(
        kern,
        grid=(nb, spb),
        in_specs=[pl.BlockSpec((ts, d), lambda b, s: (b * spb + s, 0))]
                 + [_const_spec(v) for v in (g, w_in, cw, cb, wg, br, bi, lam)],
        out_specs=[
            pl.BlockSpec((ts, c), lambda b, s: (b * spb + s, 0)),
            pl.BlockSpec((ts, mw), lambda b, s: (b * spb + s, 0)),
        ],
        out_shape=[jax.ShapeDtypeStruct((t, c), BF16), jax.ShapeDtypeStruct((t, mw), F32)],
        scratch_shapes=[
            pltpu.VMEM((ts, d), BF16),
            pltpu.VMEM((SUBLANES, c), F32),
            pltpu.VMEM((SUBLANES, c), F32),
        ],
        compiler_params=_params(("arbitrary", "arbitrary")),
        name="lru_layer",
    )(x, g, w_in, cw, cb, wg, br, bi, lam)


def _pair_gate_weights(w_r, w_i):
    n, bd, _ = w_r.shape
    z = jnp.zeros((n // 2, bd, bd), w_r.dtype)

    def blockdiag(w):
        w = w.reshape(n // 2, 2, bd, bd)
        top = jnp.concatenate([w[:, 0], z], axis=2)
        bot = jnp.concatenate([z, w[:, 1]], axis=2)
        return jnp.concatenate([top, bot], axis=1)

    return jnp.concatenate([blockdiag(w_r), blockdiag(w_i)], axis=2)


def _mix_out_kernel(tok_ref, qm_ref, mk_ref, mv_ref, wo32_ref, h_ref, g_ref, o_ref, wo_ref, *, tok_width):
    @pl.when(pl.program_id(0) == 0)
    def _():
        wo_ref[...] = wo32_ref[...].astype(BF16)

    scale = LOG2_E * HEAD_DIM ** -0.5
    y = _dot(tok_ref[...], wo_ref[0:tok_width, :])
    heads = []
    for hh in range(MEM_HEADS):
        sl = slice(hh * HEAD_DIM, (hh + 1) * HEAD_DIM)
        q = (qm_ref[:, sl] * scale).astype(BF16)
        s = _dot_nt(q, mk_ref[0, 0, :, sl])
        p = jnp.exp2(s - jnp.max(s, axis=-1, keepdims=True))
        pv = _dot(p.astype(BF16), mv_ref[0, 0, :, 2 * hh * HEAD_DIM:(2 * hh + 2) * HEAD_DIM])
        heads.append((pv[:, :HEAD_DIM] / pv[:, HEAD_DIM:]).astype(BF16))
    y = y + _dot(jnp.concatenate(heads, axis=1), wo_ref[tok_width:, :])
    o_ref[...] = h_ref[...] + _rms(y, g_ref[...])


def _mix_out(tok, qm, mk, mv, layer, w_o, h, g_post, seq, tm):
    t, tok_width = tok.shape
    d = h.shape[1]
    ml, mw = mk.shape[2], mk.shape[3]
    bpb = seq // tm
    kern = functools.partial(_mix_out_kernel, tok_width=tok_width)
    return pl.pallas_call(
        kern,
        grid=(t // tm,),
        in_specs=[
            pl.BlockSpec((tm, tok_width), lambda i: (i, 0)),
            pl.BlockSpec((tm, mw), lambda i: (i, 0)),
            pl.BlockSpec((1, 1, ml, mw), lambda i: (layer, i // bpb, 0, 0)),
            pl.BlockSpec((1, 1, ml, 2 * mw), lambda i: (layer, i // bpb, 0, 0)),
            pl.BlockSpec((None, tok_width + mw, d), lambda i: (layer, 0, 0), pipeline_mode=pl.Buffered(1)),
            pl.BlockSpec((tm, d), lambda i: (i, 0)),
            pl.BlockSpec((1, d), lambda i: (0, 0)),
        ],
        out_specs=pl.BlockSpec((tm, d), lambda i: (i, 0)),
        out_shape=jax.ShapeDtypeStruct((t, d), F32),
        scratch_shapes=[pltpu.VMEM((tok_width + mw, d), BF16)],
        compiler_params=_params(("arbitrary",)),
        name="mix_out",
    )(tok, qm, mk, mv, w_o, h, g_post.reshape(1, d))


def _mlp_kernel(h_ref, gpre_ref, w1_ref, w2_ref, gpost_ref, o_ref, hn_ref, *, rc):
    j = pl.program_id(1)
    last = pl.num_programs(1) - 1
    tm = h_ref.shape[0]

    def ffn(hn, w):
        f = _dot(hn, w[0])
        return _dot(jnp.square(jnp.maximum(f, 0.0)).astype(BF16), w[1])

    def w2_block():
        return w1_ref[...].astype(BF16), w2_ref[...].astype(BF16)

    @pl.when(j == 0)
    def _():
        w2 = w2_block()
        for r in range(tm // rc):
            rows = pl.ds(r * rc, rc)
            hn = _rms(h_ref[rows, :], gpre_ref[...]).astype(BF16)
            hn_ref[rows, :] = hn
            o_ref[rows, :] = ffn(hn, w2)

    @pl.when(jnp.logical_and(j > 0, j < last))
    def _():
        o_ref[...] += ffn(hn_ref[...], w2_block())

    @pl.when(j == last)
    def _():
        w2 = w2_block()
        for r in range(tm // rc):
            rows = pl.ds(r * rc, rc)
            acc = o_ref[rows, :] + ffn(hn_ref[rows, :], w2)
            o_ref[rows, :] = h_ref[rows, :] + _rms(acc, gpost_ref[...])


def _mlp(h, g_pre, w1, w2, layer, g_post, tm, tf):
    t, d = h.shape
    dff = w1.shape[2]
    assert dff // tf >= 2
    return pl.pallas_call(
        functools.partial(_mlp_kernel, rc=512),
        grid=(t // tm, dff // tf),
        in_specs=[
            pl.BlockSpec((tm, d), lambda i, j: (i, 0)),
            pl.BlockSpec((1, d), lambda i, j: (0, 0)),
            pl.BlockSpec((None, d, tf), lambda i, j: (layer, 0, j)),
            pl.BlockSpec((None, tf, d), lambda i, j: (layer, j, 0)),
            pl.BlockSpec((1, d), lambda i, j: (0, 0)),
        ],
        out_specs=pl.BlockSpec((tm, d), lambda i, j: (i, 0)),
        out_shape=jax.ShapeDtypeStruct((t, d), F32),
        scratch_shapes=[pltpu.VMEM((tm, d), BF16)],
        compiler_params=_params(("arbitrary", "arbitrary"), vmem=MLP_VMEM_LIMIT),
        name="mlp",
    )(h, g_pre.reshape(1, d), w1, w2, g_post.reshape(1, d))


def _swap_halves(x, lane):
    return jnp.where(lane % ROPE_DIM < ROPE_DIM // 2, pltpu.roll(x, 128 - ROPE_DIM // 2, 1),
                     pltpu.roll(x, ROPE_DIM // 2, 1))


def _proj_b_kernel(h_ref, pos_ref, freq_ref, gkv_ref, gmix_ref, wdown_ref, glat_ref, wupk_ref, wupv_ref,
                   win_ref, gqa_ref, wqn_ref, wqp_ref,
                   q_ref, k_ref, v_ref, qm_ref, *, n_heads, kv_rank, q_rank, rc):
    for r in range(h_ref.shape[0] // rc):
        _proj_b_rows(pl.ds(r * rc, rc), rc, h_ref, pos_ref, freq_ref, gkv_ref, gmix_ref, wdown_ref, glat_ref,
                     wupk_ref, wupv_ref, win_ref, gqa_ref, wqn_ref, wqp_ref, q_ref, k_ref, v_ref, qm_ref,
                     n_heads, kv_rank, q_rank)


def _proj_b_rows(rw, tm, h_ref, pos_ref, freq_ref, gkv_ref, gmix_ref, wdown_ref, glat_ref, wupk_ref, wupv_ref,
                 win_ref, gqa_ref, wqn_ref, wqp_ref, q_ref, k_ref, v_ref, qm_ref, n_heads, kv_rank, q_rank):
    h = h_ref[rw, :]
    xn = h * lax.rsqrt(jnp.mean(h * h, axis=-1, keepdims=True) + EPS)

    lane = lax.broadcasted_iota(jnp.int32, (tm, 128), 1)
    ang = pos_ref[rw, :].astype(F32) * freq_ref[...]
    cos = jnp.cos(ang)
    sin = jnp.where(lane % ROPE_DIM < ROPE_DIM // 2, -jnp.sin(ang), jnp.sin(ang))
    low = lane < ROPE_DIM

    def rope(x):
        return x * cos + _swap_halves(x, lane) * sin

    def kv_path(after):
        ckv = _dot((xn * gkv_ref[...]).astype(BF16), wdown_ref[...])
        latent = (_rms(ckv[:, :kv_rank], glat_ref[...])
                  + jnp.concatenate([after] * (kv_rank // 128), axis=1)).astype(BF16)
        kpe_raw = jnp.concatenate([ckv[:, kv_rank:], jnp.zeros((tm, 128 - ROPE_DIM), F32)], axis=1)
        kpe = jnp.where(low, rope(kpe_raw), 0.0).astype(BF16)
        k_nope = _dot(latent, wupk_ref[...])
        v = _dot(latent, wupv_ref[...])
        ones = jnp.ones((tm, HEAD_DIM), BF16)
        for hh in range(n_heads):
            sl = slice(hh * HEAD_DIM, (hh + 1) * HEAD_DIM)
            k_ref[rw, hh * QK_PAD:hh * QK_PAD + HEAD_DIM] = k_nope[:, sl].astype(BF16)
            k_ref[rw, hh * QK_PAD + HEAD_DIM:(hh + 1) * QK_PAD] = kpe
            v_ref[rw, 2 * hh * HEAD_DIM:(2 * hh + 1) * HEAD_DIM] = v[:, sl].astype(BF16)
            v_ref[rw, (2 * hh + 1) * HEAD_DIM:(2 * hh + 2) * HEAD_DIM] = ones

    def q_path():
        proj = _dot((xn * gmix_ref[...]).astype(BF16), win_ref[...])
        qm_ref[rw, :] = proj[:, q_rank:]
        cq = _rms(proj[:, :q_rank], gqa_ref[...]).astype(BF16)
        scale = LOG2_E * QK_HEAD_DIM ** -0.5
        q_nope = _dot(cq, wqn_ref[...]) * scale
        q_pe = _dot(cq, wqp_ref[...]) * scale
        for hp in range(n_heads // 2):
            pe = rope(q_pe[:, hp * 128:(hp + 1) * 128])
            for e in range(2):
                hh = 2 * hp + e
                q_ref[rw, hh * QK_PAD:hh * QK_PAD + HEAD_DIM] = q_nope[:, hh * HEAD_DIM:(hh + 1) * HEAD_DIM].astype(BF16)
                pe_h = pe if e == 0 else pltpu.roll(pe, ROPE_DIM, 1)
                q_ref[rw, hh * QK_PAD + HEAD_DIM:(hh + 1) * QK_PAD] = jnp.where(low, pe_h, 0.0).astype(BF16)
        return pe

    kv_path(_zero_from(cos + sin))
    q_path()


def _proj_b(h, pos, freq, g_kv, g_mix, w_down, g_lat, w_upk, w_upv, w_in, g_qa, w_qn, w_qp, tm):
    t, d = h.shape
    kv_rank = w_upk.shape[0]
    q_rank = w_qn.shape[0]
    n_heads = w_upk.shape[1] // HEAD_DIM
    mw = w_in.shape[1] - q_rank
    g_kv, g_mix, g_lat, g_qa = (g_kv.reshape(1, -1), g_mix.reshape(1, -1), g_lat.reshape(1, -1),
                                g_qa.reshape(1, -1))
    consts = (freq, g_kv, g_mix, w_down, g_lat, w_upk, w_upv, w_in, g_qa, w_qn, w_qp)
    kern = functools.partial(_proj_b_kernel, n_heads=n_heads, kv_rank=kv_rank, q_rank=q_rank, rc=256)
    return pl.pallas_call(
        kern,
        grid=(t // tm,),
        in_specs=[
            pl.BlockSpec((tm, d), lambda i: (i, 0)),
            pl.BlockSpec((tm, 1), lambda i: (i, 0)),
        ] + [_const_spec(a) for a in consts],
        out_specs=[
            pl.BlockSpec((tm, n_heads * QK_PAD), lambda i: (i, 0)),
            pl.BlockSpec((tm, n_heads * QK_PAD), lambda i: (i, 0)),
            pl.BlockSpec((tm, 2 * n_heads * HEAD_DIM), lambda i: (i, 0)),
            pl.BlockSpec((tm, mw), lambda i: (i, 0)),
        ],
        out_shape=[
            jax.ShapeDtypeStruct((t, n_heads * QK_PAD), BF16),
            jax.ShapeDtypeStruct((t, n_heads * QK_PAD), BF16),
            jax.ShapeDtypeStruct((t, 2 * n_heads * HEAD_DIM), BF16),
            jax.ShapeDtypeStruct((t, mw), F32),
        ],
        compiler_params=_params(("arbitrary",)),
        name="proj_b",
    )(h, pos, *consts)


def _mla_kernel(q_ref, k_ref, v_ref, o_ref, sa_ref, sb_ref, m_ref, acc_ref, *, tq, nq, hps):
    m_ref[...] = jnp.full_like(m_ref, -1e30)
    acc_ref[...] = jnp.zeros_like(acc_ref)

    def rows(blk):
        return pl.ds(pl.multiple_of(blk * tq, tq), tq)

    def scores(hd, qi, kb, s_ref):
        qk = slice(hd * QK_PAD, (hd + 1) * QK_PAD)
        s_ref[hd] = _dot_nt(q_ref[rows(qi), qk], k_ref[rows(kb), qk])

    def consume(hd, qi, kb, s_ref, visible):
        diagonal = visible is not None
        s = s_ref[hd]
        if diagonal:
            s = jnp.where(visible, s, -1e30)
        m_old = m_ref[hd]
        m_new = jnp.maximum(m_old, jnp.max(s, axis=-1, keepdims=True))
        alpha = jnp.exp2(m_old - m_new)
        p = jnp.exp2(s - jnp.concatenate([m_new] * (tq // 128), axis=1))
        pv = _dot(p.astype(BF16), v_ref[rows(kb), 2 * hd * HEAD_DIM:2 * (hd + 1) * HEAD_DIM])
        acc = jnp.concatenate([alpha, alpha], axis=1) * acc_ref[hd] + pv
        if diagonal:
            o_ref[rows(qi), hd * HEAD_DIM:(hd + 1) * HEAD_DIM] = (acc[:, :HEAD_DIM] / acc[:, HEAD_DIM:]).astype(BF16)
            m_ref[hd] = jnp.full((tq, 128), -1e30, F32)
            acc_ref[hd] = jnp.zeros((tq, 2 * HEAD_DIM), F32)
        else:
            acc_ref[hd] = acc
            m_ref[hd] = m_new

    def following(qi, kb):
        is_diag = kb == qi
        return jnp.where(is_diag, qi + 1, qi), jnp.where(is_diag, 0, kb + 1)

    for hd in range(hps):
        scores(hd, 0, 0, sa_ref)

    def pair(_, c):
        q0, k0 = c
        q1, k1 = following(q0, k0)
        q2, k2 = following(q1, k1)
        q2c = jnp.minimum(q2, nq - 1)
        d0, d1 = k0 == q0, k1 == q1

        for v0 in (False, True):
            for v1 in (False, True):
                @pl.when(jnp.logical_and(d0 == v0, d1 == v1))
                def _(v0=v0, v1=v1):
                    visible = None
                    if v0 or v1:
                        qc = lax.broadcasted_iota(jnp.int32, (tq, 1), 0) // CHUNK
                        kc = lax.broadcasted_iota(jnp.int32, (1, tq), 1) // CHUNK
                        visible = kc <= qc
                    for hd in range(hps):
                        scores(hd, q1, k1, sb_ref)
                    for hd in range(hps):
                        consume(hd, q0, k0, sa_ref, visible if v0 else None)
                    for hd in range(hps):
                        scores(hd, q2c, k2, sa_ref)
                    for hd in range(hps):
                        consume(hd, q1, k1, sb_ref, visible if v1 else None)

        return q2, k2

    n_items = nq * (nq + 1) // 2
    lax.fori_loop(0, n_items // 2, pair, (jnp.int32(0), jnp.int32(0)))


def _mla(q, k, v, nb, seq, n_heads, tq, hps):
    t = q.shape[0]
    nq = seq // tq
    assert (nq * (nq + 1) // 2) % 2 == 0
    assert n_heads % hps == 0
    kern = functools.partial(_mla_kernel, tq=tq, nq=nq, hps=hps)
    return pl.pallas_call(
        kern,
        grid=(nb, n_heads // hps),
        in_specs=[
            pl.BlockSpec((seq, hps * QK_PAD), lambda b, h: (b, h)),
            pl.BlockSpec((seq, hps * QK_PAD), lambda b, h: (b, h)),
            pl.BlockSpec((seq, hps * 2 * HEAD_DIM), lambda b, h: (b, h)),
        ],
        out_specs=pl.BlockSpec((seq, hps * HEAD_DIM), lambda b, h: (b, h)),
        out_shape=jax.ShapeDtypeStruct((t, n_heads * HEAD_DIM), BF16),
        scratch_shapes=[
            pltpu.VMEM((hps, tq, tq), F32),
            pltpu.VMEM((hps, tq, tq), F32),
            pltpu.VMEM((hps, tq, 128), F32),
            pltpu.VMEM((hps, tq, 2 * HEAD_DIM), F32),
        ],
        compiler_params=_params(("arbitrary", "arbitrary")),
        name="mla",
    )(q, k, v)


def kernel(x, mem, positions, g_mix_pre, g_mix_post, g_mlp_pre, g_mlp_post, g_mem, w_mem_k, w_mem_v, w_o, w_ff1, w_ff2, a_w_in, a_conv_w, a_conv_b, a_w_rgate, a_b_rgate, a_w_igate, a_b_igate, a_lambda, b_w_in, b_g_qa, b_w_qb, kv_g_in, kv_w_down, kv_g_latent, kv_w_up):
    nb, seq, d = x.shape
    t = nb * seq
    depth = g_mix_pre.shape[0]
    n_a = a_w_in.shape[0]
    lru_width = a_conv_w.shape[2]
    n_heads = lru_width // HEAD_DIM
    kv_rank = kv_g_latent.shape[0]
    q_rank = b_g_qa.shape[1]
    assert kv_w_down.shape[1] == kv_rank + ROPE_DIM
    assert seq % 512 == 0 and t % 1024 == 0 and d % 128 == 0
    assert depth - n_a == 1, "the shared K/V projection is fused into the single MLA layer's projections"

    bf = lambda w: w.astype(BF16)
    mk, mv = _mem_kv(mem, g_mem, bf(w_mem_k), bf(w_mem_v))

    half = ROPE_DIM // 2
    inv_freq = ROPE_THETA ** (-jnp.arange(half, dtype=F32) / half)
    freq = jnp.tile(inv_freq, 128 // half).reshape(1, 128)
    pos = positions.reshape(t, 1)

    w_up = bf(kv_w_up).reshape(kv_rank, n_heads, 2, HEAD_DIM)
    w_upk = w_up[:, :, 0, :].reshape(kv_rank, n_heads * HEAD_DIM)
    w_upv = w_up[:, :, 1, :].reshape(kv_rank, n_heads * HEAD_DIM)

    h = x.reshape(t, d)
    for layer in range(depth):
        if layer < n_a:
            la = layer
            wg = _pair_gate_weights(bf(a_w_rgate[la]), bf(a_w_igate[la]))
            tok, qm = _lru_layer(h, g_mix_pre[layer], bf(a_w_in[la]), nb, seq, a_conv_w[la], a_conv_b[la], wg,
                                 a_b_rgate[la], a_b_igate[la], a_lambda[la], ts=512)
        else:
            lb = layer - n_a
            w_qb = bf(b_w_qb[lb]).reshape(q_rank, n_heads, QK_HEAD_DIM)
            w_qn = w_qb[:, :, :HEAD_DIM].reshape(q_rank, n_heads * HEAD_DIM)
            w_qp = w_qb[:, :, HEAD_DIM:].reshape(q_rank, n_heads * ROPE_DIM)
            q, k, v, qm = _proj_b(h, pos, freq, kv_g_in, g_mix_pre[layer], bf(kv_w_down), kv_g_latent,
                                  w_upk, w_upv, bf(b_w_in[lb]), b_g_qa[lb], w_qn, w_qp, tm=512)
            tok = _mla(q, k, v, nb, seq, n_heads, tq=512, hps=2)
        h = _mix_out(tok, qm, mk, mv, layer, w_o, h, g_mix_post[layer], seq, tm=512)
        h = _mlp(h, g_mlp_pre[layer], w_ff1, w_ff2, layer, g_mlp_post[layer], tm=1024, tf=512)
    return h.reshape(nb, seq, d)
```

```python
import functools

import jax
import jax.numpy as jnp
from jax import lax
from jax.experimental import pallas as pl
from jax.experimental.pallas import tpu as pltpu

EPS = 1e-6
LRU_C = 8.0
CHUNK = 64
ROPE_THETA = 10000.0
LOG2_E = 1.4426950408889634
F32_TINY = 1.1754943508222875e-38
MEM_HEADS = 4
HEAD_DIM = 128
ROPE_DIM = 64
QK_HEAD_DIM = HEAD_DIM + ROPE_DIM
QK_PAD = 256
SUBLANES = 8
VMEM_LIMIT = 56 * 1024 * 1024
MLP_VMEM_LIMIT = 61 * 1024 * 1024

BF16 = jnp.bfloat16
F32 = jnp.float32


def _params(sem, vmem=VMEM_LIMIT):
    return pltpu.CompilerParams(dimension_semantics=sem, vmem_limit_bytes=vmem)


def _const_spec(a):
    return pl.BlockSpec(a.shape, lambda *_: (0,) * a.ndim, pipeline_mode=pl.Buffered(1))


def _rms(x, g):
    return x * lax.rsqrt(jnp.mean(x * x, axis=-1, keepdims=True) + EPS) * g


def _dot(a, b):
    return jnp.dot(a, b, preferred_element_type=F32)


def _zero_from(x):
    bits = lax.bitcast_convert_type(x, jnp.int32)
    return lax.shift_right_logical(lax.shift_right_logical(bits, 16), 16).astype(F32)


def _dot_nt(a, b):
    return lax.dot_general(a, b, (((1,), (1,)), ((), ())), preferred_element_type=F32)


def _mem_kv_kernel(mem_ref, g_ref, wk_ref, wv_ref, mk_ref, mv_ref):
    nb, ml = mk_ref.shape[1], mk_ref.shape[2]
    mn = _rms(mem_ref[...], g_ref[0]).astype(BF16)
    mk_ref[0] = _dot(mn, wk_ref[0].astype(BF16)).astype(BF16).reshape(nb, ml, -1)
    mv = _dot(mn, wv_ref[0].astype(BF16)).astype(BF16).reshape(nb, ml, -1)
    ones = jnp.ones((nb, ml, HEAD_DIM), BF16)
    for hh in range(MEM_HEADS):
        mv_ref[0, :, :, 2 * hh * HEAD_DIM:(2 * hh + 1) * HEAD_DIM] = mv[:, :, hh * HEAD_DIM:(hh + 1) * HEAD_DIM]
        mv_ref[0, :, :, (2 * hh + 1) * HEAD_DIM:(2 * hh + 2) * HEAD_DIM] = ones


def _mem_kv(mem, g_mem, w_k, w_v):
    nb, ml, d = mem.shape
    nl, _, mw = w_k.shape
    assert mw == MEM_HEADS * HEAD_DIM
    return pl.pallas_call(
        _mem_kv_kernel,
        grid=(nl,),
        in_specs=[
            pl.BlockSpec((nb * ml, d), lambda l: (0, 0)),
            pl.BlockSpec((1, 1, d), lambda l: (l, 0, 0)),
            pl.BlockSpec((1, d, mw), lambda l: (l, 0, 0)),
            pl.BlockSpec((1, d, mw), lambda l: (l, 0, 0)),
        ],
        out_specs=[
            pl.BlockSpec((1, nb, ml, mw), lambda l: (l, 0, 0, 0)),
            pl.BlockSpec((1, nb, ml, 2 * mw), lambda l: (l, 0, 0, 0)),
        ],
        out_shape=[jax.ShapeDtypeStruct((nl, nb, ml, mw), BF16),
                   jax.ShapeDtypeStruct((nl, nb, ml, 2 * mw), BF16)],
        compiler_params=_params(("arbitrary",)),
        name="mem_kv",
    )(mem.reshape(nb * ml, d), g_mem.reshape(nl, 1, d), w_k, w_v)


def _lru_layer_kernel(x_ref, g_ref, w_ref, cw_ref, cb_ref, wg_ref, br_ref, bi_ref, lam_ref,
                      tok_ref, qm_ref, hn_ref, tail_ref, carry_ref, *, ts, conv_width):
    c = cw_ref.shape[1]
    pw = 2 * HEAD_DIM

    @pl.when(pl.program_id(1) == 0)
    def _():
        tail_ref[...] = jnp.zeros_like(tail_ref)
        carry_ref[...] = jnp.zeros_like(carry_ref)

    first = []
    for r in range(2):
        rw = pl.ds(r * (ts // 2), ts // 2)
        hn_ref[rw, :] = _rms(x_ref[rw, :], g_ref[...]).astype(BF16)
        first.append(_dot(hn_ref[rw, :], w_ref[:, 0:pw]))
    neg_c_sp = -LRU_C * jax.nn.softplus(-lam_ref[...])
    row = lax.broadcasted_iota(jnp.int32, (SUBLANES, pw), 0)

    def sigmoid(z):
        return 0.5 * jnp.tanh(0.5 * z) + 0.5

    def x_branch(p):
        return _dot(hn_ref[...], w_ref[:, p * pw:(p + 1) * pw])

    xb_next = jnp.concatenate(first, axis=0)
    for p in range(c // pw):
        sl = slice(p * pw, (p + 1) * pw)
        xb = xb_next
        if p + 1 < c // pw:
            xb_next = x_branch(p + 1)
        tail = tail_ref[:, sl]
        xc = cb_ref[:, sl] + cw_ref[conv_width - 1:conv_width, sl] * xb
        for j in range(1, conv_width):
            shifted = pltpu.roll(xb, j, 0)
            top = jnp.where(row < j, pltpu.roll(tail, j, 0), shifted[:SUBLANES, :])
            shifted = jnp.concatenate([top, shifted[SUBLANES:, :]], axis=0)
            xc = xc + cw_ref[conv_width - 1 - j:conv_width - j, sl] * shifted
        tail_ref[:, sl] = xb[ts - SUBLANES:, :]
        ri = _dot(xc.astype(BF16), wg_ref[p])
        r = sigmoid(ri[:, :pw] + br_ref[:, sl])
        i = sigmoid(ri[:, pw:] + bi_ref[:, sl])
        log_a = r * neg_c_sp[:, sl]
        a_all = jnp.exp(log_a)
        y = jnp.tanh(-log_a) * (1.0 + a_all * a_all)
        b_all = (y * lax.rsqrt(jnp.maximum(y, F32_TINY))) * (i * xc)

        carry = carry_ref[:, sl]
        hs = []
        for g in range(ts // SUBLANES):
            a = a_all[g * SUBLANES:(g + 1) * SUBLANES, :]
            b = b_all[g * SUBLANES:(g + 1) * SUBLANES, :]
            for d in (1, 2, 4):
                keep = row >= d
                a_sh = pltpu.roll(a, d, 0)
                b_sh = pltpu.roll(b, d, 0)
                b = jnp.where(keep, a * b_sh + b, b)
                a = jnp.where(keep, a * a_sh, a)
            h = b + a * carry
            hs.append(h)
            carry = jnp.broadcast_to(h[SUBLANES - 1:SUBLANES, :], (SUBLANES, pw))
        carry_ref[:, sl] = carry

        gate = _dot(hn_ref[...], w_ref[:, c + p * pw:c + (p + 1) * pw])
        tok_ref[:, sl] = (jnp.concatenate(hs, axis=0) * jax.nn.gelu(gate)).astype(BF16)
    qm_ref[...] = _dot(hn_ref[...], w_ref[:, 2 * c:])


def _lru_layer(x, g, w_in, nb, seq, cw, cb, wg, br, bi, lam, ts):
    t, d = x.shape
    c = cw.shape[1]
    mw = w_in.shape[1] - 2 * c
    spb = seq // ts
    g, cb, br, bi, lam = (v.reshape(1, -1) for v in (g, cb, br, bi, lam))
    kern = functools.partial(_lru_layer_kernel, ts=ts, conv_width=cw.shape[0])
    return pl.pallas_call(
        kern,
        grid=(nb, spb),
        in_specs=[pl.BlockSpec((ts, d), lambda b, s: (b * spb + s, 0))]
                 + [_const_spec(v) for v in (g, w_in, cw, cb, wg, br, bi, lam)],
        out_specs=[
            pl.BlockSpec((ts, c), lambda b, s: (b * spb + s, 0)),
            pl.BlockSpec((ts, mw), lambda b, s: (b * spb + s, 0)),
        ],
        out_shape=[jax.ShapeDtypeStruct((t, c), BF16), jax.ShapeDtypeStruct((t, mw), F32)],
        scratch_shapes=[
            pltpu.VMEM((ts, d), BF16),
            pltpu.VMEM((SUBLANES, c), F32),
            pltpu.VMEM((SUBLANES, c), F32),
        ],
        compiler_params=_params(("arbitrary", "arbitrary")),
        name="lru_layer",
    )(x, g, w_in, cw, cb, wg, br, bi, lam)


def _pair_gate_weights(w_r, w_i):
    n, bd, _ = w_r.shape
    z = jnp.zeros((n // 2, bd, bd), w_r.dtype)

    def blockdiag(w):
        w = w.reshape(n // 2, 2, bd, bd)
        top = jnp.concatenate([w[:, 0], z], axis=2)
        bot = jnp.concatenate([z, w[:, 1]], axis=2)
        return jnp.concatenate([top, bot], axis=1)

    return jnp.concatenate([blockdiag(w_r), blockdiag(w_i)], axis=2)


def _mix_out_kernel(tok_ref, qm_ref, mk_ref, mv_ref, wo32_ref, h_ref, g_ref, o_ref, wo_ref, *, tok_width):
    @pl.when(pl.program_id(0) == 0)
    def _():
        wo_ref[...] = wo32_ref[...].astype(BF16)

    scale = LOG2_E * HEAD_DIM ** -0.5
    y = _dot(tok_ref[...], wo_ref[0:tok_width, :])
    heads = []
    for hh in range(MEM_HEADS):
        sl = slice(hh * HEAD_DIM, (hh + 1) * HEAD_DIM)
        q = (qm_ref[:, sl] * scale).astype(BF16)
        s = _dot_nt(q, mk_ref[0, 0, :, sl])
        p = jnp.exp2(s - jnp.max(s, axis=-1, keepdims=True))
        pv = _dot(p.astype(BF16), mv_ref[0, 0, :, 2 * hh * HEAD_DIM:(2 * hh + 2) * HEAD_DIM])
        heads.append((pv[:, :HEAD_DIM] / pv[:, HEAD_DIM:]).astype(BF16))
    y = y + _dot(jnp.concatenate(heads, axis=1), wo_ref[tok_width:, :])
    o_ref[...] = h_ref[...] + _rms(y, g_ref[...])


def _mix_out(tok, qm, mk, mv, layer, w_o, h, g_post, seq, tm):
    t, tok_width = tok.shape
    d = h.shape[1]
    ml, mw = mk.shape[2], mk.shape[3]
    bpb = seq // tm
    kern = functools.partial(_mix_out_kernel, tok_width=tok_width)
    return pl.pallas_call(
        kern,
        grid=(t // tm,),
        in_specs=[
            pl.BlockSpec((tm, tok_width), lambda i: (i, 0)),
            pl.BlockSpec((tm, mw), lambda i: (i, 0)),
            pl.BlockSpec((1, 1, ml, mw), lambda i: (layer, i // bpb, 0, 0)),
            pl.BlockSpec((1, 1, ml, 2 * mw), lambda i: (layer, i // bpb, 0, 0)),
            pl.BlockSpec((None, tok_width + mw, d), lambda i: (layer, 0, 0), pipeline_mode=pl.Buffered(1)),
            pl.BlockSpec((tm, d), lambda i: (i, 0)),
            pl.BlockSpec((1, d), lambda i: (0, 0)),
        ],
        out_specs=pl.BlockSpec((tm, d), lambda i: (i, 0)),
        out_shape=jax.ShapeDtypeStruct((t, d), F32),
        scratch_shapes=[pltpu.VMEM((tok_width + mw, d), BF16)],
        compiler_params=_params(("arbitrary",)),
        name="mix_out",
    )(tok, qm, mk, mv, w_o, h, g_post.reshape(1, d))


def _mlp_kernel(h_ref, gpre_ref, w1_ref, w2_ref, gpost_ref, o_ref, hn_ref, *, rc):
    j = pl.program_id(1)
    last = pl.num_programs(1) - 1
    tm = h_ref.shape[0]

    def ffn(hn, w):
        f = _dot(hn, w[0])
        return _dot(jnp.square(jnp.maximum(f, 0.0)).astype(BF16), w[1])

    def w2_block():
        return w1_ref[...].astype(BF16), w2_ref[...].astype(BF16)

    @pl.when(j == 0)
    def _():
        w2 = w2_block()
        for r in range(tm // rc):
            rows = pl.ds(r * rc, rc)
            hn = _rms(h_ref[rows, :], gpre_ref[...]).astype(BF16)
            hn_ref[rows, :] = hn
            o_ref[rows, :] = ffn(hn, w2)

    @pl.when(jnp.logical_and(j > 0, j < last))
    def _():
        o_ref[...] += ffn(hn_ref[...], w2_block())

    @pl.when(j == last)
    def _():
        w2 = w2_block()
        for r in range(tm // rc):
            rows = pl.ds(r * rc, rc)
            acc = o_ref[rows, :] + ffn(hn_ref[rows, :], w2)
            o_ref[rows, :] = h_ref[rows, :] + _rms(acc, gpost_ref[...])


def _mlp(h, g_pre, w1, w2, layer, g_post, tm, tf):
    t, d = h.shape
    dff = w1.shape[2]
    assert dff // tf >= 2
    return pl.pallas_call(
        functools.partial(_mlp_kernel, rc=512),
        grid=(t // tm, dff // tf),
        in_specs=[
            pl.BlockSpec((tm, d), lambda i, j: (i, 0)),
            pl.BlockSpec((1, d), lambda i, j: (0, 0)),
            pl.BlockSpec((None, d, tf), lambda i, j: (layer, 0, j)),
            pl.BlockSpec((None, tf, d), lambda i, j: (layer, j, 0)),
            pl.BlockSpec((1, d), lambda i, j: (0, 0)),
        ],
        out_specs=pl.BlockSpec((tm, d), lambda i, j: (i, 0)),
        out_shape=jax.ShapeDtypeStruct((t, d), F32),
        scratch_shapes=[pltpu.VMEM((tm, d), BF16)],
        compiler_params=_params(("arbitrary", "arbitrary"), vmem=MLP_VMEM_LIMIT),
        name="mlp",
    )(h, g_pre.reshape(1, d), w1, w2, g_post.reshape(1, d))


def _swap_halves(x, lane):
    return jnp.where(lane % ROPE_DIM < ROPE_DIM // 2, pltpu.roll(x, 128 - ROPE_DIM // 2, 1),
                     pltpu.roll(x, ROPE_DIM // 2, 1))


def _proj_b_kernel(h_ref, pos_ref, freq_ref, gkv_ref, gmix_ref, wdown_ref, glat_ref, wup_ref,
                   win_ref, gqa_ref, wqn_ref, wqp_ref,
                   q_ref, k_ref, v_ref, qm_ref, *, n_heads, kv_rank, q_rank, rc):
    for r in range(h_ref.shape[0] // rc):
        _proj_b_rows(pl.ds(r * rc, rc), rc, h_ref, pos_ref, freq_ref, gkv_ref, gmix_ref, wdown_ref, glat_ref,
                     wup_ref, win_ref, gqa_ref, wqn_ref, wqp_ref, q_ref, k_ref, v_ref, qm_ref,
                     n_heads, kv_rank, q_rank)


def _proj_b_rows(rw, tm, h_ref, pos_ref, freq_ref, gkv_ref, gmix_ref, wdown_ref, glat_ref, wup_ref,
                 win_ref, gqa_ref, wqn_ref, wqp_ref, q_ref, k_ref, v_ref, qm_ref, n_heads, kv_rank, q_rank):
    h = h_ref[rw, :]
    xn = h * lax.rsqrt(jnp.mean(h * h, axis=-1, keepdims=True) + EPS)

    lane = lax.broadcasted_iota(jnp.int32, (tm, 128), 1)
    ang = pos_ref[rw, :].astype(F32) * freq_ref[...]
    cos = jnp.cos(ang)
    sin = jnp.where(lane % ROPE_DIM < ROPE_DIM // 2, -jnp.sin(ang), jnp.sin(ang))
    low = lane < ROPE_DIM

    def rope(x):
        return x * cos + _swap_halves(x, lane) * sin

    def kv_path(after):
        ckv = _dot((xn * gkv_ref[...]).astype(BF16), wdown_ref[...])
        latent = (_rms(ckv[:, :kv_rank], glat_ref[...])
                  + jnp.concatenate([after] * (kv_rank // 128), axis=1)).astype(BF16)
        kpe_raw = jnp.concatenate([ckv[:, kv_rank:], jnp.zeros((tm, 128 - ROPE_DIM), F32)], axis=1)
        kpe = jnp.where(low, rope(kpe_raw), 0.0).astype(BF16)
        kv = _dot(latent, wup_ref[...])
        ones = jnp.ones((tm, HEAD_DIM), BF16)
        for hh in range(n_heads):
            k_nope = kv[:, 2 * hh * HEAD_DIM:(2 * hh + 1) * HEAD_DIM]
            v = kv[:, (2 * hh + 1) * HEAD_DIM:(2 * hh + 2) * HEAD_DIM]
            k_ref[rw, hh * QK_PAD:hh * QK_PAD + HEAD_DIM] = k_nope.astype(BF16)
            k_ref[rw, hh * QK_PAD + HEAD_DIM:(hh + 1) * QK_PAD] = kpe
            v_ref[rw, 2 * hh * HEAD_DIM:(2 * hh + 1) * HEAD_DIM] = v.astype(BF16)
            v_ref[rw, (2 * hh + 1) * HEAD_DIM:(2 * hh + 2) * HEAD_DIM] = ones

    def q_path():
        proj = _dot((xn * gmix_ref[...]).astype(BF16), win_ref[...])
        qm_ref[rw, :] = proj[:, q_rank:]
        cq = _rms(proj[:, :q_rank], gqa_ref[...]).astype(BF16)
        scale = LOG2_E * QK_HEAD_DIM ** -0.5
        q_nope = _dot(cq, wqn_ref[...]) * scale
        q_pe = _dot(cq, wqp_ref[...]) * scale
        for hp in range(n_heads // 2):
            pe = rope(q_pe[:, hp * 128:(hp + 1) * 128])
            for e in range(2):
                hh = 2 * hp + e
                q_ref[rw, hh * QK_PAD:hh * QK_PAD + HEAD_DIM] = q_nope[:, hh * HEAD_DIM:(hh + 1) * HEAD_DIM].astype(BF16)
                pe_h = pe if e == 0 else pltpu.roll(pe, ROPE_DIM, 1)
                q_ref[rw, hh * QK_PAD + HEAD_DIM:(hh + 1) * QK_PAD] = jnp.where(low, pe_h, 0.0).astype(BF16)
        return pe

    kv_path(_zero_from(cos + sin))
    q_path()


def _proj_b(h, pos, freq, g_kv, g_mix, w_down, g_lat, w_up, w_in, g_qa, w_qn, w_qp, tm):
    t, d = h.shape
    kv_rank = w_up.shape[0]
    q_rank = w_qn.shape[0]
    n_heads = w_up.shape[1] // (2 * HEAD_DIM)
    mw = w_in.shape[1] - q_rank
    g_kv, g_mix, g_lat, g_qa = (g_kv.reshape(1, -1), g_mix.reshape(1, -1), g_lat.reshape(1, -1),
                                g_qa.reshape(1, -1))
    consts = (freq, g_kv, g_mix, w_down, g_lat, w_up, w_in, g_qa, w_qn, w_qp)
    kern = functools.partial(_proj_b_kernel, n_heads=n_heads, kv_rank=kv_rank, q_rank=q_rank, rc=256)
    return pl.pallas_call(
        kern,
        grid=(t // tm,),
        in_specs=[
            pl.BlockSpec((tm, d), lambda i: (i, 0)),
            pl.BlockSpec((tm, 1), lambda i: (i, 0)),
        ] + [_const_spec(a) for a in consts],
        out_specs=[
            pl.BlockSpec((tm, n_heads * QK_PAD), lambda i: (i, 0)),
            pl.BlockSpec((tm, n_heads * QK_PAD), lambda i: (i, 0)),
            pl.BlockSpec((tm, 2 * n_heads * HEAD_DIM), lambda i: (i, 0)),
            pl.BlockSpec((tm, mw), lambda i: (i, 0)),
        ],
        out_shape=[
            jax.ShapeDtypeStruct((t, n_heads * QK_PAD), BF16),
            jax.ShapeDtypeStruct((t, n_heads * QK_PAD), BF16),
            jax.ShapeDtypeStruct((t, 2 * n_heads * HEAD_DIM), BF16),
            jax.ShapeDtypeStruct((t, mw), F32),
        ],
        compiler_params=_params(("arbitrary",)),
        name="proj_b",
    )(h, pos, *consts)


def _mla_kernel(q_ref, k_ref, v_ref, o_ref, sa_ref, sb_ref, m_ref, acc_ref, *, tq, nq, hps):
    m_ref[...] = jnp.full_like(m_ref, -1e30)
    acc_ref[...] = jnp.zeros_like(acc_ref)

    def rows(blk):
        return pl.ds(pl.multiple_of(blk * tq, tq), tq)

    def scores(hd, qi, kb, s_ref):
        qk = slice(hd * QK_PAD, (hd + 1) * QK_PAD)
        s_ref[hd] = _dot_nt(q_ref[rows(qi), qk], k_ref[rows(kb), qk])

    def consume(hd, qi, kb, s_ref, visible):
        diagonal = visible is not None
        s = s_ref[hd]
        if diagonal:
            s = jnp.where(visible, s, -1e30)
        m_old = m_ref[hd]
        m_new = jnp.maximum(m_old, jnp.max(s, axis=-1, keepdims=True))
        alpha = jnp.exp2(m_old - m_new)
        p = jnp.exp2(s - jnp.concatenate([m_new] * (tq // 128), axis=1))
        pv = _dot(p.astype(BF16), v_ref[rows(kb), 2 * hd * HEAD_DIM:2 * (hd + 1) * HEAD_DIM])
        acc = jnp.concatenate([alpha, alpha], axis=1) * acc_ref[hd] + pv
        if diagonal:
            o_ref[rows(qi), hd * HEAD_DIM:(hd + 1) * HEAD_DIM] = (acc[:, :HEAD_DIM] / acc[:, HEAD_DIM:]).astype(BF16)
            m_ref[hd] = jnp.full((tq, 128), -1e30, F32)
            acc_ref[hd] = jnp.zeros((tq, 2 * HEAD_DIM), F32)
        else:
            acc_ref[hd] = acc
            m_ref[hd] = m_new

    def following(qi, kb):
        is_diag = kb == qi
        return jnp.where(is_diag, qi + 1, qi), jnp.where(is_diag, 0, kb + 1)

    for hd in range(hps):
        scores(hd, 0, 0, sa_ref)

    def pair(_, c):
        q0, k0 = c
        q1, k1 = following(q0, k0)
        q2, k2 = following(q1, k1)
        q2c = jnp.minimum(q2, nq - 1)
        d0, d1 = k0 == q0, k1 == q1

        for v0 in (False, True):
            for v1 in (False, True):
                @pl.when(jnp.logical_and(d0 == v0, d1 == v1))
                def _(v0=v0, v1=v1):
                    visible = None
                    if v0 or v1:
                        qc = lax.broadcasted_iota(jnp.int32, (tq, 1), 0) // CHUNK
                        kc = lax.broadcasted_iota(jnp.int32, (1, tq), 1) // CHUNK
                        visible = kc <= qc
                    for hd in range(hps):
                        scores(hd, q1, k1, sb_ref)
                    for hd in range(hps):
                        consume(hd, q0, k0, sa_ref, visible if v0 else None)
                    for hd in range(hps):
                        scores(hd, q2c, k2, sa_ref)
                    for hd in range(hps):
                        consume(hd, q1, k1, sb_ref, visible if v1 else None)

        return q2, k2

    n_items = nq * (nq + 1) // 2
    lax.fori_loop(0, n_items // 2, pair, (jnp.int32(0), jnp.int32(0)))


def _mla(q, k, v, nb, seq, n_heads, tq, hps):
    t = q.shape[0]
    nq = seq // tq
    assert (nq * (nq + 1) // 2) % 2 == 0
    assert n_heads % hps == 0
    kern = functools.partial(_mla_kernel, tq=tq, nq=nq, hps=hps)
    return pl.pallas_call(
        kern,
        grid=(nb, n_heads // hps),
        in_specs=[
            pl.BlockSpec((seq, hps * QK_PAD), lambda b, h: (b, h)),
            pl.BlockSpec((seq, hps * QK_PAD), lambda b, h: (b, h)),
            pl.BlockSpec((seq, hps * 2 * HEAD_DIM), lambda b, h: (b, h)),
        ],
        out_specs=pl.BlockSpec((seq, hps * HEAD_DIM), lambda b, h: (b, h)),
        out_shape=jax.ShapeDtypeStruct((t, n_heads * HEAD_DIM), BF16),
        scratch_shapes=[
            pltpu.VMEM((hps, tq, tq), F32),
            pltpu.VMEM((hps, tq, tq), F32),
            pltpu.VMEM((hps, tq, 128), F32),
            pltpu.VMEM((hps, tq, 2 * HEAD_DIM), F32),
        ],
        compiler_params=_params(("arbitrary", "arbitrary")),
        name="mla",
    )(q, k, v)


def kernel(x, mem, positions, g_mix_pre, g_mix_post, g_mlp_pre, g_mlp_post, g_mem, w_mem_k, w_mem_v, w_o, w_ff1, w_ff2, a_w_in, a_conv_w, a_conv_b, a_w_rgate, a_b_rgate, a_w_igate, a_b_igate, a_lambda, b_w_in, b_g_qa, b_w_qb, kv_g_in, kv_w_down, kv_g_latent, kv_w_up):
    nb, seq, d = x.shape
    t = nb * seq
    depth = g_mix_pre.shape[0]
    n_a = a_w_in.shape[0]
    lru_width = a_conv_w.shape[2]
    n_heads = lru_width // HEAD_DIM
    kv_rank = kv_g_latent.shape[0]
    q_rank = b_g_qa.shape[1]
    assert kv_w_down.shape[1] == kv_rank + ROPE_DIM
    assert seq % 512 == 0 and t % 1024 == 0 and d % 128 == 0
    assert depth - n_a == 1, "the shared K/V projection is fused into the single MLA layer's projections"

    bf = lambda w: w.astype(BF16)
    mk, mv = _mem_kv(mem, g_mem, w_mem_k, w_mem_v)

    half = ROPE_DIM // 2
    inv_freq = ROPE_THETA ** (-jnp.arange(half, dtype=F32) / half)
    freq = jnp.tile(inv_freq, 128 // half).reshape(1, 128)
    pos = positions.reshape(t, 1)


    h = x.reshape(t, d)
    for layer in range(depth):
        if layer < n_a:
            la = layer
            wg = _pair_gate_weights(bf(a_w_rgate[la]), bf(a_w_igate[la]))
            tok, qm = _lru_layer(h, g_mix_pre[layer], bf(a_w_in[la]), nb, seq, a_conv_w[la], a_conv_b[la], wg,
                                 a_b_rgate[la], a_b_igate[la], a_lambda[la], ts=512)
        else:
            lb = layer - n_a
            w_qb = bf(b_w_qb[lb]).reshape(q_rank, n_heads, QK_HEAD_DIM)
            w_qn = w_qb[:, :, :HEAD_DIM].reshape(q_rank, n_heads * HEAD_DIM)
            w_qp = w_qb[:, :, HEAD_DIM:].reshape(q_rank, n_heads * ROPE_DIM)
            q, k, v, qm = _proj_b(h, pos, freq, kv_g_in, g_mix_pre[layer], bf(kv_w_down), kv_g_latent,
                                  bf(kv_w_up), bf(b_w_in[lb]), b_g_qa[lb], w_qn, w_qp, tm=512)
            tok = _mla(q, k, v, nb, seq, n_heads, tq=512, hps=2)
        h = _mix_out(tok, qm, mk, mv, layer, w_o, h, g_mix_post[layer], seq, tm=512)
        h = _mlp(h, g_mlp_pre[layer], w_ff1, w_ff2, layer, g_mlp_post[layer], tm=1024, tf=512)
    return h.reshape(nb, seq, d)
```

```python
import functools

import jax
import jax.numpy as jnp
from jax import lax
from jax.experimental import pallas as pl
from jax.experimental.pallas import tpu as pltpu

EPS = 1e-6
LRU_C = 8.0
CHUNK = 64
ROPE_THETA = 10000.0
LOG2_E = 1.4426950408889634
F32_TINY = 1.1754943508222875e-38
MEM_HEADS = 4
HEAD_DIM = 128
ROPE_DIM = 64
QK_HEAD_DIM = HEAD_DIM + ROPE_DIM
QK_PAD = 256
SUBLANES = 8
VMEM_LIMIT = 56 * 1024 * 1024
MLP_VMEM_LIMIT = 61 * 1024 * 1024

BF16 = jnp.bfloat16
F32 = jnp.float32


def _params(sem, vmem=VMEM_LIMIT):
    return pltpu.CompilerParams(dimension_semantics=sem, vmem_limit_bytes=vmem)


def _const_spec(a):
    return pl.BlockSpec(a.shape, lambda *_: (0,) * a.ndim, pipeline_mode=pl.Buffered(1))


def _rms(x, g):
    return x * lax.rsqrt(jnp.mean(x * x, axis=-1, keepdims=True) + EPS) * g


def _dot(a, b):
    return jnp.dot(a, b, preferred_element_type=F32)


def _zero_from(x):
    bits = lax.bitcast_convert_type(x, jnp.int32)
    return lax.shift_right_logical(lax.shift_right_logical(bits, 16), 16).astype(F32)


def _dot_nt(a, b):
    return lax.dot_general(a, b, (((1,), (1,)), ((), ())), preferred_element_type=F32)


def _mem_kv_kernel(mem_ref, g_ref, wk_ref, wv_ref, mk_ref, mv_ref):
    nb, ml = mk_ref.shape[1], mk_ref.shape[2]
    mn = _rms(mem_ref[...], g_ref[0]).astype(BF16)
    mk_ref[0] = _dot(mn, wk_ref[0].astype(BF16)).astype(BF16).reshape(nb, ml, -1)
    mv = _dot(mn, wv_ref[0].astype(BF16)).astype(BF16).reshape(nb, ml, -1)
    ones = jnp.ones((nb, ml, HEAD_DIM), BF16)
    for hh in range(MEM_HEADS):
        mv_ref[0, :, :, 2 * hh * HEAD_DIM:(2 * hh + 1) * HEAD_DIM] = mv[:, :, hh * HEAD_DIM:(hh + 1) * HEAD_DIM]
        mv_ref[0, :, :, (2 * hh + 1) * HEAD_DIM:(2 * hh + 2) * HEAD_DIM] = ones


def _mem_kv(mem, g_mem, w_k, w_v):
    nb, ml, d = mem.shape
    nl, _, mw = w_k.shape
    assert mw == MEM_HEADS * HEAD_DIM
    return pl.pallas_call(
        _mem_kv_kernel,
        grid=(nl,),
        in_specs=[
            pl.BlockSpec((nb * ml, d), lambda l: (0, 0)),
            pl.BlockSpec((1, 1, d), lambda l: (l, 0, 0)),
            pl.BlockSpec((1, d, mw), lambda l: (l, 0, 0)),
            pl.BlockSpec((1, d, mw), lambda l: (l, 0, 0)),
        ],
        out_specs=[
            pl.BlockSpec((1, nb, ml, mw), lambda l: (l, 0, 0, 0)),
            pl.BlockSpec((1, nb, ml, 2 * mw), lambda l: (l, 0, 0, 0)),
        ],
        out_shape=[jax.ShapeDtypeStruct((nl, nb, ml, mw), BF16),
                   jax.ShapeDtypeStruct((nl, nb, ml, 2 * mw), BF16)],
        compiler_params=_params(("arbitrary",)),
        name="mem_kv",
    )(mem.reshape(nb * ml, d), g_mem.reshape(nl, 1, d), w_k, w_v)


def _lru_layer_kernel(x_ref, g_ref, w_ref, cw_ref, cb_ref, wg_ref, br_ref, bi_ref, lam_ref,
                      tok_ref, qm_ref, hn_ref, tail_ref, carry_ref, *, ts, conv_width):
    c = cw_ref.shape[1]
    pw = 2 * HEAD_DIM

    @pl.when(pl.program_id(1) == 0)
    def _():
        tail_ref[...] = jnp.zeros_like(tail_ref)
        carry_ref[...] = jnp.zeros_like(carry_ref)

    first = []
    for r in range(2):
        rw = pl.ds(r * (ts // 2), ts // 2)
        hn_ref[rw, :] = _rms(x_ref[rw, :], g_ref[...]).astype(BF16)
        first.append(_dot(hn_ref[rw, :], w_ref[:, 0:pw]))
    neg_c_sp = -LRU_C * jax.nn.softplus(-lam_ref[...])
    row = lax.broadcasted_iota(jnp.int32, (SUBLANES, pw), 0)

    def sigmoid(z):
        return 0.5 * jnp.tanh(0.5 * z) + 0.5

    def x_branch(p):
        return _dot(hn_ref[...], w_ref[:, p * pw:(p + 1) * pw])

    n_blocks = c // pw

    def gates(p, xb):
        sl = slice(p * pw, (p + 1) * pw)
        tail = tail_ref[:, sl]
        xc = cb_ref[:, sl] + cw_ref[conv_width - 1:conv_width, sl] * xb
        for j in range(1, conv_width):
            shifted = pltpu.roll(xb, j, 0)
            top = jnp.where(row < j, pltpu.roll(tail, j, 0), shifted[:SUBLANES, :])
            shifted = jnp.concatenate([top, shifted[SUBLANES:, :]], axis=0)
            xc = xc + cw_ref[conv_width - 1 - j:conv_width - j, sl] * shifted
        tail_ref[:, sl] = xb[ts - SUBLANES:, :]
        ri = _dot(xc.astype(BF16), wg_ref[p])
        r = sigmoid(ri[:, :pw] + br_ref[:, sl])
        i = sigmoid(ri[:, pw:] + bi_ref[:, sl])
        log_a = r * neg_c_sp[:, sl]
        a_all = jnp.exp(log_a)
        y = jnp.tanh(-log_a) * (1.0 + a_all * a_all)
        b_all = (y * lax.rsqrt(jnp.maximum(y, F32_TINY))) * (i * xc)
        gate = _dot(hn_ref[...], w_ref[:, c + p * pw:c + (p + 1) * pw])
        return a_all, b_all, gate

    def recur(p, a_all, b_all, gate):
        sl = slice(p * pw, (p + 1) * pw)
        carry = carry_ref[:, sl]
        hs = []
        for g in range(ts // SUBLANES):
            a = a_all[g * SUBLANES:(g + 1) * SUBLANES, :]
            b = b_all[g * SUBLANES:(g + 1) * SUBLANES, :]
            for d in (1, 2, 4):
                keep = row >= d
                a_sh = pltpu.roll(a, d, 0)
                b_sh = pltpu.roll(b, d, 0)
                b = jnp.where(keep, a * b_sh + b, b)
                a = jnp.where(keep, a * a_sh, a)
            h = b + a * carry
            hs.append(h)
            carry = jnp.broadcast_to(h[SUBLANES - 1:SUBLANES, :], (SUBLANES, pw))
        carry_ref[:, sl] = carry
        tok_ref[:, sl] = (jnp.concatenate(hs, axis=0) * jax.nn.gelu(gate)).astype(BF16)

    xb = jnp.concatenate(first, axis=0)
    pending = None
    for p in range(n_blocks):
        xb_next = x_branch(p + 1) if p + 1 < n_blocks else None
        staged = gates(p, xb)
        if pending is not None:
            recur(p - 1, *pending)
        pending, xb = staged, xb_next
    recur(n_blocks - 1, *pending)
    qm_ref[...] = _dot(hn_ref[...], w_ref[:, 2 * c:])


def _lru_layer(x, g, w_in, nb, seq, cw, cb, wg, br, bi, lam, ts):
    t, d = x.shape
    c = cw.shape[1]
    mw = w_in.shape[1] - 2 * c
    spb = seq // ts
    g, cb, br, bi, lam = (v.reshape(1, -1) for v in (g, cb, br, bi, lam))
    kern = functools.partial(_lru_layer_kernel, ts=ts, conv_width=cw.shape[0])
    return pl.pallas_call(
        kern,
        grid=(nb, spb),
        in_specs=[pl.BlockSpec((ts, d), lambda b, s: (b * spb + s, 0))]
                 + [_const_spec(v) for v in (g, w_in, cw, cb, wg, br, bi, lam)],
        out_specs=[
            pl.BlockSpec((ts, c), lambda b, s: (b * spb + s, 0)),
            pl.BlockSpec((ts, mw), lambda b, s: (b * spb + s, 0)),
        ],
        out_shape=[jax.ShapeDtypeStruct((t, c), BF16), jax.ShapeDtypeStruct((t, mw), F32)],
        scratch_shapes=[
            pltpu.VMEM((ts, d), BF16),
            pltpu.VMEM((SUBLANES, c), F32),
            pltpu.VMEM((SUBLANES, c), F32),
        ],
        compiler_params=_params(("arbitrary", "arbitrary")),
        name="lru_layer",
    )(x, g, w_in, cw, cb, wg, br, bi, lam)


def _pair_gate_weights(w_r, w_i):
    n, bd, _ = w_r.shape
    z = jnp.zeros((n // 2, bd, bd), w_r.dtype)

    def blockdiag(w):
        w = w.reshape(n // 2, 2, bd, bd)
        top = jnp.concatenate([w[:, 0], z], axis=2)
        bot = jnp.concatenate([z, w[:, 1]], axis=2)
        return jnp.concatenate([top, bot], axis=1)

    return jnp.concatenate([blockdiag(w_r), blockdiag(w_i)], axis=2)


def _mix_out_kernel(tok_ref, qm_ref, mk_ref, mv_ref, wo32_ref, h_ref, g_ref, o_ref, wo_ref, *, tok_width):
    @pl.when(pl.program_id(0) == 0)
    def _():
        wo_ref[...] = wo32_ref[...].astype(BF16)

    scale = LOG2_E * HEAD_DIM ** -0.5
    y = _dot(tok_ref[...], wo_ref[0:tok_width, :])
    heads = []
    for hh in range(MEM_HEADS):
        sl = slice(hh * HEAD_DIM, (hh + 1) * HEAD_DIM)
        q = (qm_ref[:, sl] * scale).astype(BF16)
        s = _dot_nt(q, mk_ref[0, 0, :, sl])
        p = jnp.exp2(s - jnp.max(s, axis=-1, keepdims=True))
        pv = _dot(p.astype(BF16), mv_ref[0, 0, :, 2 * hh * HEAD_DIM:(2 * hh + 2) * HEAD_DIM])
        heads.append((pv[:, :HEAD_DIM] / pv[:, HEAD_DIM:]).astype(BF16))
    y = y + _dot(jnp.concatenate(heads, axis=1), wo_ref[tok_width:, :])
    o_ref[...] = h_ref[...] + _rms(y, g_ref[...])


def _mix_out(tok, qm, mk, mv, layer, w_o, h, g_post, seq, tm):
    t, tok_width = tok.shape
    d = h.shape[1]
    ml, mw = mk.shape[2], mk.shape[3]
    bpb = seq // tm
    kern = functools.partial(_mix_out_kernel, tok_width=tok_width)
    return pl.pallas_call(
        kern,
        grid=(t // tm,),
        in_specs=[
            pl.BlockSpec((tm, tok_width), lambda i: (i, 0)),
            pl.BlockSpec((tm, mw), lambda i: (i, 0)),
            pl.BlockSpec((1, 1, ml, mw), lambda i: (layer, i // bpb, 0, 0)),
            pl.BlockSpec((1, 1, ml, 2 * mw), lambda i: (layer, i // bpb, 0, 0)),
            pl.BlockSpec((None, tok_width + mw, d), lambda i: (layer, 0, 0), pipeline_mode=pl.Buffered(1)),
            pl.BlockSpec((tm, d), lambda i: (i, 0)),
            pl.BlockSpec((1, d), lambda i: (0, 0)),
        ],
        out_specs=pl.BlockSpec((tm, d), lambda i: (i, 0)),
        out_shape=jax.ShapeDtypeStruct((t, d), F32),
        scratch_shapes=[pltpu.VMEM((tok_width + mw, d), BF16)],
        compiler_params=_params(("arbitrary",)),
        name="mix_out",
    )(tok, qm, mk, mv, w_o, h, g_post.reshape(1, d))


def _mlp_kernel(h_ref, gpre_ref, w1_ref, w2_ref, gpost_ref, o_ref, hn_ref, *, rc):
    j = pl.program_id(1)
    last = pl.num_programs(1) - 1
    tm = h_ref.shape[0]

    def ffn(hn, w):
        f = _dot(hn, w[0])
        return _dot(jnp.square(jnp.maximum(f, 0.0)).astype(BF16), w[1])

    def w2_block():
        return w1_ref[...].astype(BF16), w2_ref[...].astype(BF16)

    @pl.when(j == 0)
    def _():
        w2 = w2_block()
        for r in range(tm // rc):
            rows = pl.ds(r * rc, rc)
            hn = _rms(h_ref[rows, :], gpre_ref[...]).astype(BF16)
            hn_ref[rows, :] = hn
            o_ref[rows, :] = ffn(hn, w2)

    @pl.when(jnp.logical_and(j > 0, j < last))
    def _():
        o_ref[...] += ffn(hn_ref[...], w2_block())

    @pl.when(j == last)
    def _():
        w2 = w2_block()
        for r in range(tm // rc):
            rows = pl.ds(r * rc, rc)
            acc = o_ref[rows, :] + ffn(hn_ref[rows, :], w2)
            o_ref[rows, :] = h_ref[rows, :] + _rms(acc, gpost_ref[...])


def _mlp(h, g_pre, w1, w2, layer, g_post, tm, tf):
    t, d = h.shape
    dff = w1.shape[2]
    assert dff // tf >= 2
    return pl.pallas_call(
        functools.partial(_mlp_kernel, rc=512),
        grid=(t // tm, dff // tf),
        in_specs=[
            pl.BlockSpec((tm, d), lambda i, j: (i, 0)),
            pl.BlockSpec((1, d), lambda i, j: (0, 0)),
            pl.BlockSpec((None, d, tf), lambda i, j: (layer, 0, j)),
            pl.BlockSpec((None, tf, d), lambda i, j: (layer, j, 0)),
            pl.BlockSpec((1, d), lambda i, j: (0, 0)),
        ],
        out_specs=pl.BlockSpec((tm, d), lambda i, j: (i, 0)),
        out_shape=jax.ShapeDtypeStruct((t, d), F32),
        scratch_shapes=[pltpu.VMEM((tm, d), BF16)],
        compiler_params=_params(("arbitrary", "arbitrary"), vmem=MLP_VMEM_LIMIT),
        name="mlp",
    )(h, g_pre.reshape(1, d), w1, w2, g_post.reshape(1, d))


def _swap_halves(x, lane):
    return jnp.where(lane % ROPE_DIM < ROPE_DIM // 2, pltpu.roll(x, 128 - ROPE_DIM // 2, 1),
                     pltpu.roll(x, ROPE_DIM // 2, 1))


def _proj_b_kernel(h_ref, pos_ref, freq_ref, gkv_ref, gmix_ref, wdown_ref, glat_ref, wup_ref,
                   win_ref, gqa_ref, wqn_ref, wqp_ref,
                   q_ref, k_ref, v_ref, qm_ref, *, n_heads, kv_rank, q_rank, rc):
    for r in range(h_ref.shape[0] // rc):
        _proj_b_rows(pl.ds(r * rc, rc), rc, h_ref, pos_ref, freq_ref, gkv_ref, gmix_ref, wdown_ref, glat_ref,
                     wup_ref, win_ref, gqa_ref, wqn_ref, wqp_ref, q_ref, k_ref, v_ref, qm_ref,
                     n_heads, kv_rank, q_rank)


def _proj_b_rows(rw, tm, h_ref, pos_ref, freq_ref, gkv_ref, gmix_ref, wdown_ref, glat_ref, wup_ref,
                 win_ref, gqa_ref, wqn_ref, wqp_ref, q_ref, k_ref, v_ref, qm_ref, n_heads, kv_rank, q_rank):
    h = h_ref[rw, :]
    xn = h * lax.rsqrt(jnp.mean(h * h, axis=-1, keepdims=True) + EPS)

    lane = lax.broadcasted_iota(jnp.int32, (tm, 128), 1)
    ang = pos_ref[rw, :].astype(F32) * freq_ref[...]
    cos = jnp.cos(ang)
    sin = jnp.where(lane % ROPE_DIM < ROPE_DIM // 2, -jnp.sin(ang), jnp.sin(ang))
    low = lane < ROPE_DIM

    def rope(x):
        return x * cos + _swap_halves(x, lane) * sin

    def kv_path(after):
        ckv = _dot((xn * gkv_ref[...]).astype(BF16), wdown_ref[...])
        latent = (_rms(ckv[:, :kv_rank], glat_ref[...])
                  + jnp.concatenate([after] * (kv_rank // 128), axis=1)).astype(BF16)
        kpe_raw = jnp.concatenate([ckv[:, kv_rank:], jnp.zeros((tm, 128 - ROPE_DIM), F32)], axis=1)
        kpe = jnp.where(low, rope(kpe_raw), 0.0).astype(BF16)
        kv = _dot(latent, wup_ref[...])
        ones = jnp.ones((tm, HEAD_DIM), BF16)
        for hh in range(n_heads):
            k_nope = kv[:, 2 * hh * HEAD_DIM:(2 * hh + 1) * HEAD_DIM]
            v = kv[:, (2 * hh + 1) * HEAD_DIM:(2 * hh + 2) * HEAD_DIM]
            k_ref[rw, hh * QK_PAD:hh * QK_PAD + HEAD_DIM] = k_nope.astype(BF16)
            k_ref[rw, hh * QK_PAD + HEAD_DIM:(hh + 1) * QK_PAD] = kpe
            v_ref[rw, 2 * hh * HEAD_DIM:(2 * hh + 1) * HEAD_DIM] = v.astype(BF16)
            v_ref[rw, (2 * hh + 1) * HEAD_DIM:(2 * hh + 2) * HEAD_DIM] = ones

    def q_path():
        proj = _dot((xn * gmix_ref[...]).astype(BF16), win_ref[...])
        qm_ref[rw, :] = proj[:, q_rank:]
        cq = _rms(proj[:, :q_rank], gqa_ref[...]).astype(BF16)
        scale = LOG2_E * QK_HEAD_DIM ** -0.5
        q_nope = _dot(cq, wqn_ref[...]) * scale
        q_pe = _dot(cq, wqp_ref[...]) * scale
        for hp in range(n_heads // 2):
            pe = rope(q_pe[:, hp * 128:(hp + 1) * 128])
            for e in range(2):
                hh = 2 * hp + e
                q_ref[rw, hh * QK_PAD:hh * QK_PAD + HEAD_DIM] = q_nope[:, hh * HEAD_DIM:(hh + 1) * HEAD_DIM].astype(BF16)
                pe_h = pe if e == 0 else pltpu.roll(pe, ROPE_DIM, 1)
                q_ref[rw, hh * QK_PAD + HEAD_DIM:(hh + 1) * QK_PAD] = jnp.where(low, pe_h, 0.0).astype(BF16)
        return pe

    kv_path(_zero_from(cos + sin))
    q_path()


def _proj_b(h, pos, freq, g_kv, g_mix, w_down, g_lat, w_up, w_in, g_qa, w_qn, w_qp, tm):
    t, d = h.shape
    kv_rank = w_up.shape[0]
    q_rank = w_qn.shape[0]
    n_heads = w_up.shape[1] // (2 * HEAD_DIM)
    mw = w_in.shape[1] - q_rank
    g_kv, g_mix, g_lat, g_qa = (g_kv.reshape(1, -1), g_mix.reshape(1, -1), g_lat.reshape(1, -1),
                                g_qa.reshape(1, -1))
    consts = (freq, g_kv, g_mix, w_down, g_lat, w_up, w_in, g_qa, w_qn, w_qp)
    kern = functools.partial(_proj_b_kernel, n_heads=n_heads, kv_rank=kv_rank, q_rank=q_rank, rc=256)
    return pl.pallas_call(
        kern,
        grid=(t // tm,),
        in_specs=[
            pl.BlockSpec((tm, d), lambda i: (i, 0)),
            pl.BlockSpec((tm, 1), lambda i: (i, 0)),
        ] + [_const_spec(a) for a in consts],
        out_specs=[
            pl.BlockSpec((tm, n_heads * QK_PAD), lambda i: (i, 0)),
            pl.BlockSpec((tm, n_heads * QK_PAD), lambda i: (i, 0)),
            pl.BlockSpec((tm, 2 * n_heads * HEAD_DIM), lambda i: (i, 0)),
            pl.BlockSpec((tm, mw), lambda i: (i, 0)),
        ],
        out_shape=[
            jax.ShapeDtypeStruct((t, n_heads * QK_PAD), BF16),
            jax.ShapeDtypeStruct((t, n_heads * QK_PAD), BF16),
            jax.ShapeDtypeStruct((t, 2 * n_heads * HEAD_DIM), BF16),
            jax.ShapeDtypeStruct((t, mw), F32),
        ],
        compiler_params=_params(("arbitrary",)),
        name="proj_b",
    )(h, pos, *consts)


def _mla_kernel(q_ref, k_ref, v_ref, o_ref, sa_ref, sb_ref, m_ref, acc_ref, *, tq, nq, hps):
    m_ref[...] = jnp.full_like(m_ref, -1e30)
    acc_ref[...] = jnp.zeros_like(acc_ref)

    def rows(blk):
        return pl.ds(pl.multiple_of(blk * tq, tq), tq)

    def scores(hd, qi, kb, s_ref):
        qk = slice(hd * QK_PAD, (hd + 1) * QK_PAD)
        s_ref[hd] = _dot_nt(q_ref[rows(qi), qk], k_ref[rows(kb), qk])

    def consume(hd, qi, kb, s_ref, visible):
        diagonal = visible is not None
        s = s_ref[hd]
        if diagonal:
            s = jnp.where(visible, s, -1e30)
        m_old = m_ref[hd]
        m_new = jnp.maximum(m_old, jnp.max(s, axis=-1, keepdims=True))
        alpha = jnp.exp2(m_old - m_new)
        p = jnp.exp2(s - jnp.concatenate([m_new] * (tq // 128), axis=1))
        pv = _dot(p.astype(BF16), v_ref[rows(kb), 2 * hd * HEAD_DIM:2 * (hd + 1) * HEAD_DIM])
        acc = jnp.concatenate([alpha, alpha], axis=1) * acc_ref[hd] + pv
        if diagonal:
            o_ref[rows(qi), hd * HEAD_DIM:(hd + 1) * HEAD_DIM] = (acc[:, :HEAD_DIM] / acc[:, HEAD_DIM:]).astype(BF16)
            m_ref[hd] = jnp.full((tq, 128), -1e30, F32)
            acc_ref[hd] = jnp.zeros((tq, 2 * HEAD_DIM), F32)
        else:
            acc_ref[hd] = acc
            m_ref[hd] = m_new

    def following(qi, kb):
        is_diag = kb == qi
        return jnp.where(is_diag, qi + 1, qi), jnp.where(is_diag, 0, kb + 1)

    for hd in range(hps):
        scores(hd, 0, 0, sa_ref)

    def pair(_, c):
        q0, k0 = c
        q1, k1 = following(q0, k0)
        q2, k2 = following(q1, k1)
        q2c = jnp.minimum(q2, nq - 1)
        d0, d1 = k0 == q0, k1 == q1

        for v0 in (False, True):
            for v1 in (False, True):
                @pl.when(jnp.logical_and(d0 == v0, d1 == v1))
                def _(v0=v0, v1=v1):
                    visible = None
                    if v0 or v1:
                        qc = lax.broadcasted_iota(jnp.int32, (tq, 1), 0) // CHUNK
                        kc = lax.broadcasted_iota(jnp.int32, (1, tq), 1) // CHUNK
                        visible = kc <= qc
                    for hd in range(hps):
                        scores(hd, q1, k1, sb_ref)
                        consume(hd, q0, k0, sa_ref, visible if v0 else None)
                    for hd in range(hps):
                        scores(hd, q2c, k2, sa_ref)
                        consume(hd, q1, k1, sb_ref, visible if v1 else None)

        return q2, k2

    n_items = nq * (nq + 1) // 2
    lax.fori_loop(0, n_items // 2, pair, (jnp.int32(0), jnp.int32(0)))


def _mla(q, k, v, nb, seq, n_heads, tq, hps):
    t = q.shape[0]
    nq = seq // tq
    assert (nq * (nq + 1) // 2) % 2 == 0
    assert n_heads % hps == 0
    kern = functools.partial(_mla_kernel, tq=tq, nq=nq, hps=hps)
    return pl.pallas_call(
        kern,
        grid=(nb, n_heads // hps),
        in_specs=[
            pl.BlockSpec((seq, hps * QK_PAD), lambda b, h: (b, h)),
            pl.BlockSpec((seq, hps * QK_PAD), lambda b, h: (b, h)),
            pl.BlockSpec((seq, hps * 2 * HEAD_DIM), lambda b, h: (b, h)),
        ],
        out_specs=pl.BlockSpec((seq, hps * HEAD_DIM), lambda b, h: (b, h)),
        out_shape=jax.ShapeDtypeStruct((t, n_heads * HEAD_DIM), BF16),
        scratch_shapes=[
            pltpu.VMEM((hps, tq, tq), F32),
            pltpu.VMEM((hps, tq, tq), F32),
            pltpu.VMEM((hps, tq, 128), F32),
            pltpu.VMEM((hps, tq, 2 * HEAD_DIM), F32),
        ],
        compiler_params=_params(("arbitrary", "arbitrary")),
        name="mla",
    )(q, k, v)


def kernel(x, mem, positions, g_mix_pre, g_mix_post, g_mlp_pre, g_mlp_post, g_mem, w_mem_k, w_mem_v, w_o, w_ff1, w_ff2, a_w_in, a_conv_w, a_conv_b, a_w_rgate, a_b_rgate, a_w_igate, a_b_igate, a_lambda, b_w_in, b_g_qa, b_w_qb, kv_g_in, kv_w_down, kv_g_latent, kv_w_up):
    nb, seq, d = x.shape
    t = nb * seq
    depth = g_mix_pre.shape[0]
    n_a = a_w_in.shape[0]
    lru_width = a_conv_w.shape[2]
    n_heads = lru_width // HEAD_DIM
    kv_rank = kv_g_latent.shape[0]
    q_rank = b_g_qa.shape[1]
    assert kv_w_down.shape[1] == kv_rank + ROPE_DIM
    assert seq % 512 == 0 and t % 1024 == 0 and d % 128 == 0
    assert depth - n_a == 1, "the shared K/V projection is fused into the single MLA layer's projections"

    bf = lambda w: w.astype(BF16)
    mk, mv = _mem_kv(mem, g_mem, w_mem_k, w_mem_v)

    half = ROPE_DIM // 2
    inv_freq = ROPE_THETA ** (-jnp.arange(half, dtype=F32) / half)
    freq = jnp.tile(inv_freq, 128 // half).reshape(1, 128)
    pos = positions.reshape(t, 1)


    h = x.reshape(t, d)
    for layer in range(depth):
        if layer < n_a:
            la = layer
            wg = _pair_gate_weights(bf(a_w_rgate[la]), bf(a_w_igate[la]))
            tok, qm = _lru_layer(h, g_mix_pre[layer], bf(a_w_in[la]), nb, seq, a_conv_w[la], a_conv_b[la], wg,
                                 a_b_rgate[la], a_b_igate[la], a_lambda[la], ts=512)
        else:
            lb = layer - n_a
            w_qb = bf(b_w_qb[lb]).reshape(q_rank, n_heads, QK_HEAD_DIM)
            w_qn = w_qb[:, :, :HEAD_DIM].reshape(q_rank, n_heads * HEAD_DIM)
            w_qp = w_qb[:, :, HEAD_DIM:].reshape(q_rank, n_heads * ROPE_DIM)
            q, k, v, qm = _proj_b(h, pos, freq, kv_g_in, g_mix_pre[layer], bf(kv_w_down), kv_g_latent,
                                  bf(kv_w_up), bf(b_w_in[lb]), b_g_qa[lb], w_qn, w_qp, tm=512)
            tok = _mla(q, k, v, nb, seq, n_heads, tq=512, hps=2)
        h = _mix_out(tok, qm, mk, mv, layer, w_o, h, g_mix_post[layer], seq, tm=512)
        h = _mlp(h, g_mlp_pre[layer], w_ff1, w_ff2, layer, g_mlp_post[layer], tm=1024, tf=512)
    return h.reshape(nb, seq, d)
```

```python
import functools

import jax
import jax.numpy as jnp
from jax import lax
from jax.experimental import pallas as pl
from jax.experimental.pallas import tpu as pltpu

EPS = 1e-6
LRU_C = 8.0
CHUNK = 64
ROPE_THETA = 10000.0
LOG2_E = 1.4426950408889634
F32_TINY = 1.1754943508222875e-38
MEM_HEADS = 4
HEAD_DIM = 128
ROPE_DIM = 64
QK_HEAD_DIM = HEAD_DIM + ROPE_DIM
QK_PAD = 256
SUBLANES = 8
VMEM_LIMIT = 56 * 1024 * 1024
MLP_VMEM_LIMIT = 61 * 1024 * 1024

BF16 = jnp.bfloat16
F32 = jnp.float32


def _params(sem, vmem=VMEM_LIMIT):
    return pltpu.CompilerParams(dimension_semantics=sem, vmem_limit_bytes=vmem)


def _const_spec(a):
    return pl.BlockSpec(a.shape, lambda *_: (0,) * a.ndim, pipeline_mode=pl.Buffered(1))


def _rms(x, g):
    return x * lax.rsqrt(jnp.mean(x * x, axis=-1, keepdims=True) + EPS) * g


def _dot(a, b):
    return jnp.dot(a, b, preferred_element_type=F32)


def _zero_from(x):
    bits = lax.bitcast_convert_type(x, jnp.int32)
    return lax.shift_right_logical(lax.shift_right_logical(bits, 16), 16).astype(F32)


def _dot_nt(a, b):
    return lax.dot_general(a, b, (((1,), (1,)), ((), ())), preferred_element_type=F32)


def _mem_kv_kernel(mem_ref, g_ref, wk_ref, wv_ref, mk_ref, mv_ref):
    nb, ml = mk_ref.shape[1], mk_ref.shape[2]
    mn = _rms(mem_ref[...], g_ref[0]).astype(BF16)
    mk_ref[0] = _dot(mn, wk_ref[0].astype(BF16)).astype(BF16).reshape(nb, ml, -1)
    mv = _dot(mn, wv_ref[0].astype(BF16)).astype(BF16).reshape(nb, ml, -1)
    ones = jnp.ones((nb, ml, HEAD_DIM), BF16)
    for hh in range(MEM_HEADS):
        mv_ref[0, :, :, 2 * hh * HEAD_DIM:(2 * hh + 1) * HEAD_DIM] = mv[:, :, hh * HEAD_DIM:(hh + 1) * HEAD_DIM]
        mv_ref[0, :, :, (2 * hh + 1) * HEAD_DIM:(2 * hh + 2) * HEAD_DIM] = ones


def _mem_kv(mem, g_mem, w_k, w_v):
    nb, ml, d = mem.shape
    nl, _, mw = w_k.shape
    assert mw == MEM_HEADS * HEAD_DIM
    return pl.pallas_call(
        _mem_kv_kernel,
        grid=(nl,),
        in_specs=[
            pl.BlockSpec((nb * ml, d), lambda l: (0, 0)),
            pl.BlockSpec((1, 1, d), lambda l: (l, 0, 0)),
            pl.BlockSpec((1, d, mw), lambda l: (l, 0, 0)),
            pl.BlockSpec((1, d, mw), lambda l: (l, 0, 0)),
        ],
        out_specs=[
            pl.BlockSpec((1, nb, ml, mw), lambda l: (l, 0, 0, 0)),
            pl.BlockSpec((1, nb, ml, 2 * mw), lambda l: (l, 0, 0, 0)),
        ],
        out_shape=[jax.ShapeDtypeStruct((nl, nb, ml, mw), BF16),
                   jax.ShapeDtypeStruct((nl, nb, ml, 2 * mw), BF16)],
        compiler_params=_params(("arbitrary",)),
        name="mem_kv",
    )(mem.reshape(nb * ml, d), g_mem.reshape(nl, 1, d), w_k, w_v)


def _lru_layer_kernel(x_ref, g_ref, w_ref, cw_ref, cb_ref, wg_ref, br_ref, bi_ref, lam_ref,
                      tok_ref, qm_ref, hn_ref, tail_ref, carry_ref, *, ts, conv_width):
    c = cw_ref.shape[1]
    pw = 2 * HEAD_DIM

    @pl.when(pl.program_id(1) == 0)
    def _():
        tail_ref[...] = jnp.zeros_like(tail_ref)
        carry_ref[...] = jnp.zeros_like(carry_ref)

    first = []
    for r in range(2):
        rw = pl.ds(r * (ts // 2), ts // 2)
        hn_ref[rw, :] = _rms(x_ref[rw, :], g_ref[...]).astype(BF16)
        first.append(_dot(hn_ref[rw, :], w_ref[:, 0:pw]))
    neg_c_sp = -LRU_C * jax.nn.softplus(-lam_ref[...])
    row = lax.broadcasted_iota(jnp.int32, (SUBLANES, pw), 0)

    def sigmoid(z):
        return 0.5 * jnp.tanh(0.5 * z) + 0.5

    def x_branch(p):
        return _dot(hn_ref[...], w_ref[:, p * pw:(p + 1) * pw])

    n_blocks = c // pw

    def gates(p, xb):
        sl = slice(p * pw, (p + 1) * pw)
        tail = tail_ref[:, sl]
        xc = cb_ref[:, sl] + cw_ref[conv_width - 1:conv_width, sl] * xb
        for j in range(1, conv_width):
            shifted = pltpu.roll(xb, j, 0)
            top = jnp.where(row < j, pltpu.roll(tail, j, 0), shifted[:SUBLANES, :])
            shifted = jnp.concatenate([top, shifted[SUBLANES:, :]], axis=0)
            xc = xc + cw_ref[conv_width - 1 - j:conv_width - j, sl] * shifted
        tail_ref[:, sl] = xb[ts - SUBLANES:, :]
        ri = _dot(xc.astype(BF16), wg_ref[p])
        r = sigmoid(ri[:, :pw] + br_ref[:, sl])
        i = sigmoid(ri[:, pw:] + bi_ref[:, sl])
        log_a = r * neg_c_sp[:, sl]
        a_all = jnp.exp(log_a)
        y = jnp.tanh(-log_a) * (1.0 + a_all * a_all)
        b_all = (y * lax.rsqrt(jnp.maximum(y, F32_TINY))) * (i * xc)
        gate = _dot(hn_ref[...], w_ref[:, c + p * pw:c + (p + 1) * pw])
        return a_all, b_all, gate

    def recur(p, a_all, b_all, gate):
        sl = slice(p * pw, (p + 1) * pw)
        carry = carry_ref[:, sl]
        hs = []
        for g in range(ts // SUBLANES):
            a = a_all[g * SUBLANES:(g + 1) * SUBLANES, :]
            b = b_all[g * SUBLANES:(g + 1) * SUBLANES, :]
            for d in (1, 2, 4):
                keep = row >= d
                a_sh = pltpu.roll(a, d, 0)
                b_sh = pltpu.roll(b, d, 0)
                b = jnp.where(keep, a * b_sh + b, b)
                a = jnp.where(keep, a * a_sh, a)
            h = b + a * carry
            hs.append(h)
            carry = jnp.broadcast_to(h[SUBLANES - 1:SUBLANES, :], (SUBLANES, pw))
        carry_ref[:, sl] = carry
        tok_ref[:, sl] = (jnp.concatenate(hs, axis=0) * jax.nn.gelu(gate)).astype(BF16)

    xb = jnp.concatenate(first, axis=0)
    pending = None
    for p in range(n_blocks):
        xb_next = x_branch(p + 1) if p + 1 < n_blocks else None
        staged = gates(p, xb)
        if pending is not None:
            recur(p - 1, *pending)
        pending, xb = staged, xb_next
    recur(n_blocks - 1, *pending)
    qm_ref[...] = _dot(hn_ref[...], w_ref[:, 2 * c:])


def _lru_layer(x, g, w_in, nb, seq, cw, cb, wg, br, bi, lam, ts):
    t, d = x.shape
    c = cw.shape[1]
    mw = w_in.shape[1] - 2 * c
    spb = seq // ts
    g, cb, br, bi, lam = (v.reshape(1, -1) for v in (g, cb, br, bi, lam))
    kern = functools.partial(_lru_layer_kernel, ts=ts, conv_width=cw.shape[0])
    return pl.pallas_call(
        kern,
        grid=(nb, spb),
        in_specs=[pl.BlockSpec((ts, d), lambda b, s: (b * spb + s, 0))]
                 + [_const_spec(v) for v in (g, w_in, cw, cb, wg, br, bi, lam)],
        out_specs=[
            pl.BlockSpec((ts, c), lambda b, s: (b * spb + s, 0)),
            pl.BlockSpec((ts, mw), lambda b, s: (b * spb + s, 0)),
        ],
        out_shape=[jax.ShapeDtypeStruct((t, c), BF16), jax.ShapeDtypeStruct((t, mw), F32)],
        scratch_shapes=[
            pltpu.VMEM((ts, d), BF16),
            pltpu.VMEM((SUBLANES, c), F32),
            pltpu.VMEM((SUBLANES, c), F32),
        ],
        compiler_params=_params(("arbitrary", "arbitrary")),
        name="lru_layer",
    )(x, g, w_in, cw, cb, wg, br, bi, lam)


def _pair_gate_weights(w_r, w_i):
    n, bd, _ = w_r.shape
    z = jnp.zeros((n // 2, bd, bd), w_r.dtype)

    def blockdiag(w):
        w = w.reshape(n // 2, 2, bd, bd)
        top = jnp.concatenate([w[:, 0], z], axis=2)
        bot = jnp.concatenate([z, w[:, 1]], axis=2)
        return jnp.concatenate([top, bot], axis=1)

    return jnp.concatenate([blockdiag(w_r), blockdiag(w_i)], axis=2)


def _mix_out_kernel(tok_ref, qm_ref, mk_ref, mv_ref, wo32_ref, h_ref, g_ref, o_ref, wo_ref, *, tok_width):
    @pl.when(pl.program_id(0) == 0)
    def _():
        wo_ref[...] = wo32_ref[...].astype(BF16)

    scale = LOG2_E * HEAD_DIM ** -0.5
    y = _dot(tok_ref[...], wo_ref[0:tok_width, :])
    heads = []
    for hh in range(MEM_HEADS):
        sl = slice(hh * HEAD_DIM, (hh + 1) * HEAD_DIM)
        q = (qm_ref[:, sl] * scale).astype(BF16)
        s = _dot_nt(q, mk_ref[0, 0, :, sl])
        p = jnp.exp2(s - jnp.max(s, axis=-1, keepdims=True))
        pv = _dot(p.astype(BF16), mv_ref[0, 0, :, 2 * hh * HEAD_DIM:(2 * hh + 2) * HEAD_DIM])
        heads.append((pv[:, :HEAD_DIM] / pv[:, HEAD_DIM:]).astype(BF16))
    y = y + _dot(jnp.concatenate(heads, axis=1), wo_ref[tok_width:, :])
    o_ref[...] = h_ref[...] + _rms(y, g_ref[...])


def _mix_out(tok, qm, mk, mv, layer, w_o, h, g_post, seq, tm):
    t, tok_width = tok.shape
    d = h.shape[1]
    ml, mw = mk.shape[2], mk.shape[3]
    bpb = seq // tm
    kern = functools.partial(_mix_out_kernel, tok_width=tok_width)
    return pl.pallas_call(
        kern,
        grid=(t // tm,),
        in_specs=[
            pl.BlockSpec((tm, tok_width), lambda i: (i, 0)),
            pl.BlockSpec((tm, mw), lambda i: (i, 0)),
            pl.BlockSpec((1, 1, ml, mw), lambda i: (layer, i // bpb, 0, 0)),
            pl.BlockSpec((1, 1, ml, 2 * mw), lambda i: (layer, i // bpb, 0, 0)),
            pl.BlockSpec((None, tok_width + mw, d), lambda i: (layer, 0, 0), pipeline_mode=pl.Buffered(1)),
            pl.BlockSpec((tm, d), lambda i: (i, 0)),
            pl.BlockSpec((1, d), lambda i: (0, 0)),
        ],
        out_specs=pl.BlockSpec((tm, d), lambda i: (i, 0)),
        out_shape=jax.ShapeDtypeStruct((t, d), F32),
        scratch_shapes=[pltpu.VMEM((tok_width + mw, d), BF16)],
        compiler_params=_params(("arbitrary",)),
        name="mix_out",
    )(tok, qm, mk, mv, w_o, h, g_post.reshape(1, d))


def _mlp_kernel(h_ref, gpre_ref, w1_ref, w2_ref, gpost_ref, o_ref, hn_ref, *, rc):
    j = pl.program_id(1)
    last = pl.num_programs(1) - 1
    tm = h_ref.shape[0]

    def ffn(hn, w):
        f = _dot(hn, w[0])
        return _dot(jnp.square(jnp.maximum(f, 0.0)).astype(BF16), w[1])

    def w2_block():
        return w1_ref[...].astype(BF16), w2_ref[...].astype(BF16)

    @pl.when(j == 0)
    def _():
        w2 = w2_block()
        for r in range(tm // rc):
            rows = pl.ds(r * rc, rc)
            hn = _rms(h_ref[rows, :], gpre_ref[...]).astype(BF16)
            hn_ref[rows, :] = hn
            o_ref[rows, :] = ffn(hn, w2)

    @pl.when(jnp.logical_and(j > 0, j < last))
    def _():
        o_ref[...] += ffn(hn_ref[...], w2_block())

    @pl.when(j == last)
    def _():
        w2 = w2_block()
        for r in range(tm // rc):
            rows = pl.ds(r * rc, rc)
            acc = o_ref[rows, :] + ffn(hn_ref[rows, :], w2)
            o_ref[rows, :] = h_ref[rows, :] + _rms(acc, gpost_ref[...])


def _mlp(h, g_pre, w1, w2, layer, g_post, tm, tf):
    t, d = h.shape
    dff = w1.shape[2]
    assert dff // tf >= 2
    return pl.pallas_call(
        functools.partial(_mlp_kernel, rc=512),
        grid=(t // tm, dff // tf),
        in_specs=[
            pl.BlockSpec((tm, d), lambda i, j: (i, 0)),
            pl.BlockSpec((1, d), lambda i, j: (0, 0)),
            pl.BlockSpec((None, d, tf), lambda i, j: (layer, 0, j)),
            pl.BlockSpec((None, tf, d), lambda i, j: (layer, j, 0)),
            pl.BlockSpec((1, d), lambda i, j: (0, 0)),
        ],
        out_specs=pl.BlockSpec((tm, d), lambda i, j: (i, 0)),
        out_shape=jax.ShapeDtypeStruct((t, d), F32),
        scratch_shapes=[pltpu.VMEM((tm, d), BF16)],
        compiler_params=_params(("arbitrary", "arbitrary"), vmem=MLP_VMEM_LIMIT),
        name="mlp",
    )(h, g_pre.reshape(1, d), w1, w2, g_post.reshape(1, d))


def _swap_halves(x, lane):
    return jnp.where(lane % ROPE_DIM < ROPE_DIM // 2, pltpu.roll(x, 128 - ROPE_DIM // 2, 1),
                     pltpu.roll(x, ROPE_DIM // 2, 1))


def _proj_b_kernel(h_ref, pos_ref, freq_ref, gkv_ref, gmix_ref, wdown_ref, glat_ref, wup_ref,
                   win_ref, gqa_ref, wqn_ref, wqp_ref,
                   q_ref, k_ref, v_ref, qm_ref, *, n_heads, kv_rank, q_rank, rc):
    for r in range(h_ref.shape[0] // rc):
        _proj_b_rows(pl.ds(r * rc, rc), rc, h_ref, pos_ref, freq_ref, gkv_ref, gmix_ref, wdown_ref, glat_ref,
                     wup_ref, win_ref, gqa_ref, wqn_ref, wqp_ref, q_ref, k_ref, v_ref, qm_ref,
                     n_heads, kv_rank, q_rank)


def _proj_b_rows(rw, tm, h_ref, pos_ref, freq_ref, gkv_ref, gmix_ref, wdown_ref, glat_ref, wup_ref,
                 win_ref, gqa_ref, wqn_ref, wqp_ref, q_ref, k_ref, v_ref, qm_ref, n_heads, kv_rank, q_rank):
    h = h_ref[rw, :]
    xn = h * lax.rsqrt(jnp.mean(h * h, axis=-1, keepdims=True) + EPS)

    lane = lax.broadcasted_iota(jnp.int32, (tm, 128), 1)
    ang = pos_ref[rw, :].astype(F32) * freq_ref[...]
    cos = jnp.cos(ang)
    sin = jnp.where(lane % ROPE_DIM < ROPE_DIM // 2, -jnp.sin(ang), jnp.sin(ang))
    low = lane < ROPE_DIM

    def rope(x):
        return x * cos + _swap_halves(x, lane) * sin

    def kv_path(after):
        ckv = _dot((xn * gkv_ref[...]).astype(BF16), wdown_ref[...])
        latent = (_rms(ckv[:, :kv_rank], glat_ref[...])
                  + jnp.concatenate([after] * (kv_rank // 128), axis=1)).astype(BF16)
        kpe_raw = jnp.concatenate([ckv[:, kv_rank:], jnp.zeros((tm, 128 - ROPE_DIM), F32)], axis=1)
        kpe = jnp.where(low, rope(kpe_raw), 0.0).astype(BF16)
        kv = _dot(latent, wup_ref[...])
        ones = jnp.ones((tm, HEAD_DIM), BF16)
        for hh in range(n_heads):
            k_nope = kv[:, 2 * hh * HEAD_DIM:(2 * hh + 1) * HEAD_DIM]
            v = kv[:, (2 * hh + 1) * HEAD_DIM:(2 * hh + 2) * HEAD_DIM]
            k_ref[rw, hh * QK_PAD:hh * QK_PAD + HEAD_DIM] = k_nope.astype(BF16)
            k_ref[rw, hh * QK_PAD + HEAD_DIM:(hh + 1) * QK_PAD] = kpe
            v_ref[rw, 2 * hh * HEAD_DIM:(2 * hh + 1) * HEAD_DIM] = v.astype(BF16)
            v_ref[rw, (2 * hh + 1) * HEAD_DIM:(2 * hh + 2) * HEAD_DIM] = ones

    def q_path():
        proj = _dot((xn * gmix_ref[...]).astype(BF16), win_ref[...])
        qm_ref[rw, :] = proj[:, q_rank:]
        cq = _rms(proj[:, :q_rank], gqa_ref[...]).astype(BF16)
        scale = LOG2_E * QK_HEAD_DIM ** -0.5
        q_pe = _dot(cq, wqp_ref[...]) * scale
        q_nope = _dot(cq, wqn_ref[...]) * scale
        for hp in range(n_heads // 2):
            pe = rope(q_pe[:, hp * 128:(hp + 1) * 128])
            for e in range(2):
                hh = 2 * hp + e
                q_ref[rw, hh * QK_PAD:hh * QK_PAD + HEAD_DIM] = q_nope[:, hh * HEAD_DIM:(hh + 1) * HEAD_DIM].astype(BF16)
                pe_h = pe if e == 0 else pltpu.roll(pe, ROPE_DIM, 1)
                q_ref[rw, hh * QK_PAD + HEAD_DIM:(hh + 1) * QK_PAD] = jnp.where(low, pe_h, 0.0).astype(BF16)
        return pe

    q_path()
    kv_path(_zero_from(cos + sin))


def _proj_b(h, pos, freq, g_kv, g_mix, w_down, g_lat, w_up, w_in, g_qa, w_qn, w_qp, tm):
    t, d = h.shape
    kv_rank = w_up.shape[0]
    q_rank = w_qn.shape[0]
    n_heads = w_up.shape[1] // (2 * HEAD_DIM)
    mw = w_in.shape[1] - q_rank
    g_kv, g_mix, g_lat, g_qa = (g_kv.reshape(1, -1), g_mix.reshape(1, -1), g_lat.reshape(1, -1),
                                g_qa.reshape(1, -1))
    consts = (freq, g_kv, g_mix, w_down, g_lat, w_up, w_in, g_qa, w_qn, w_qp)
    kern = functools.partial(_proj_b_kernel, n_heads=n_heads, kv_rank=kv_rank, q_rank=q_rank, rc=256)
    return pl.pallas_call(
        kern,
        grid=(t // tm,),
        in_specs=[
            pl.BlockSpec((tm, d), lambda i: (i, 0)),
            pl.BlockSpec((tm, 1), lambda i: (i, 0)),
        ] + [_const_spec(a) for a in consts],
        out_specs=[
            pl.BlockSpec((tm, n_heads * QK_PAD), lambda i: (i, 0)),
            pl.BlockSpec((tm, n_heads * QK_PAD), lambda i: (i, 0)),
            pl.BlockSpec((tm, 2 * n_heads * HEAD_DIM), lambda i: (i, 0)),
            pl.BlockSpec((tm, mw), lambda i: (i, 0)),
        ],
        out_shape=[
            jax.ShapeDtypeStruct((t, n_heads * QK_PAD), BF16),
            jax.ShapeDtypeStruct((t, n_heads * QK_PAD), BF16),
            jax.ShapeDtypeStruct((t, 2 * n_heads * HEAD_DIM), BF16),
            jax.ShapeDtypeStruct((t, mw), F32),
        ],
        compiler_params=_params(("arbitrary",)),
        name="proj_b",
    )(h, pos, *consts)


def _mla_kernel(q_ref, k_ref, v_ref, o_ref, sa_ref, sb_ref, m_ref, acc_ref, *, tq, nq, hps):
    m_ref[...] = jnp.full_like(m_ref, -1e30)
    acc_ref[...] = jnp.zeros_like(acc_ref)

    def rows(blk):
        return pl.ds(pl.multiple_of(blk * tq, tq), tq)

    def scores(hd, qi, kb, s_ref):
        qk = slice(hd * QK_PAD, (hd + 1) * QK_PAD)
        s_ref[hd] = _dot_nt(q_ref[rows(qi), qk], k_ref[rows(kb), qk])

    def consume(hd, qi, kb, s_ref, visible):
        diagonal = visible is not None
        s = s_ref[hd]
        if diagonal:
            s = jnp.where(visible, s, -1e30)
        m_old = m_ref[hd]
        m_new = jnp.maximum(m_old, jnp.max(s, axis=-1, keepdims=True))
        alpha = jnp.exp2(m_old - m_new)
        p = jnp.exp2(s - jnp.concatenate([m_new] * (tq // 128), axis=1))
        pv = _dot(p.astype(BF16), v_ref[rows(kb), 2 * hd * HEAD_DIM:2 * (hd + 1) * HEAD_DIM])
        acc = jnp.concatenate([alpha, alpha], axis=1) * acc_ref[hd] + pv
        if diagonal:
            o_ref[rows(qi), hd * HEAD_DIM:(hd + 1) * HEAD_DIM] = (acc[:, :HEAD_DIM] / acc[:, HEAD_DIM:]).astype(BF16)
            m_ref[hd] = jnp.full((tq, 128), -1e30, F32)
            acc_ref[hd] = jnp.zeros((tq, 2 * HEAD_DIM), F32)
        else:
            acc_ref[hd] = acc
            m_ref[hd] = m_new

    def following(qi, kb):
        is_diag = kb == qi
        return jnp.where(is_diag, qi + 1, qi), jnp.where(is_diag, 0, kb + 1)

    for hd in range(hps):
        scores(hd, 0, 0, sa_ref)

    def pair(_, c):
        q0, k0 = c
        q1, k1 = following(q0, k0)
        q2, k2 = following(q1, k1)
        q2c = jnp.minimum(q2, nq - 1)
        d0, d1 = k0 == q0, k1 == q1

        for v0 in (False, True):
            for v1 in (False, True):
                @pl.when(jnp.logical_and(d0 == v0, d1 == v1))
                def _(v0=v0, v1=v1):
                    visible = None
                    if v0 or v1:
                        qc = lax.broadcasted_iota(jnp.int32, (tq, 1), 0) // CHUNK
                        kc = lax.broadcasted_iota(jnp.int32, (1, tq), 1) // CHUNK
                        visible = kc <= qc
                    for hd in range(hps):
                        scores(hd, q1, k1, sb_ref)
                        consume(hd, q0, k0, sa_ref, visible if v0 else None)
                    for hd in range(hps):
                        scores(hd, q2c, k2, sa_ref)
                        consume(hd, q1, k1, sb_ref, visible if v1 else None)

        return q2, k2

    n_items = nq * (nq + 1) // 2
    lax.fori_loop(0, n_items // 2, pair, (jnp.int32(0), jnp.int32(0)))


def _mla(q, k, v, nb, seq, n_heads, tq, hps):
    t = q.shape[0]
    nq = seq // tq
    assert (nq * (nq + 1) // 2) % 2 == 0
    assert n_heads % hps == 0
    kern = functools.partial(_mla_kernel, tq=tq, nq=nq, hps=hps)
    return pl.pallas_call(
        kern,
        grid=(nb, n_heads // hps),
        in_specs=[
            pl.BlockSpec((seq, hps * QK_PAD), lambda b, h: (b, h)),
            pl.BlockSpec((seq, hps * QK_PAD), lambda b, h: (b, h)),
            pl.BlockSpec((seq, hps * 2 * HEAD_DIM), lambda b, h: (b, h)),
        ],
        out_specs=pl.BlockSpec((seq, hps * HEAD_DIM), lambda b, h: (b, h)),
        out_shape=jax.ShapeDtypeStruct((t, n_heads * HEAD_DIM), BF16),
        scratch_shapes=[
            pltpu.VMEM((hps, tq, tq), F32),
            pltpu.VMEM((hps, tq, tq), F32),
            pltpu.VMEM((hps, tq, 128), F32),
            pltpu.VMEM((hps, tq, 2 * HEAD_DIM), F32),
        ],
        compiler_params=_params(("arbitrary", "arbitrary")),
        name="mla",
    )(q, k, v)


def kernel(x, mem, positions, g_mix_pre, g_mix_post, g_mlp_pre, g_mlp_post, g_mem, w_mem_k, w_mem_v, w_o, w_ff1, w_ff2, a_w_in, a_conv_w, a_conv_b, a_w_rgate, a_b_rgate, a_w_igate, a_b_igate, a_lambda, b_w_in, b_g_qa, b_w_qb, kv_g_in, kv_w_down, kv_g_latent, kv_w_up):
    nb, seq, d = x.shape
    t = nb * seq
    depth = g_mix_pre.shape[0]
    n_a = a_w_in.shape[0]
    lru_width = a_conv_w.shape[2]
    n_heads = lru_width // HEAD_DIM
    kv_rank = kv_g_latent.shape[0]
    q_rank = b_g_qa.shape[1]
    assert kv_w_down.shape[1] == kv_rank + ROPE_DIM
    assert seq % 512 == 0 and t % 1024 == 0 and d % 128 == 0
    assert depth - n_a == 1, "the shared K/V projection is fused into the single MLA layer's projections"

    bf = lambda w: w.astype(BF16)
    mk, mv = _mem_kv(mem, g_mem, w_mem_k, w_mem_v)

    half = ROPE_DIM // 2
    inv_freq = ROPE_THETA ** (-jnp.arange(half, dtype=F32) / half)
    freq = jnp.tile(inv_freq, 128 // half).reshape(1, 128)
    pos = positions.reshape(t, 1)


    h = x.reshape(t, d)
    for layer in range(depth):
        if layer < n_a:
            la = layer
            wg = _pair_gate_weights(bf(a_w_rgate[la]), bf(a_w_igate[la]))
            tok, qm = _lru_layer(h, g_mix_pre[layer], bf(a_w_in[la]), nb, seq, a_conv_w[la], a_conv_b[la], wg,
                                 a_b_rgate[la], a_b_igate[la], a_lambda[la], ts=512)
        else:
            lb = layer - n_a
            w_qb = bf(b_w_qb[lb]).reshape(q_rank, n_heads, QK_HEAD_DIM)
            w_qn = w_qb[:, :, :HEAD_DIM].reshape(q_rank, n_heads * HEAD_DIM)
            w_qp = w_qb[:, :, HEAD_DIM:].reshape(q_rank, n_heads * ROPE_DIM)
            q, k, v, qm = _proj_b(h, pos, freq, kv_g_in, g_mix_pre[layer], bf(kv_w_down), kv_g_latent,
                                  bf(kv_w_up), bf(b_w_in[lb]), b_g_qa[lb], w_qn, w_qp, tm=512)
            tok = _mla(q, k, v, nb, seq, n_heads, tq=512, hps=2)
        h = _mix_out(tok, qm, mk, mv, layer, w_o, h, g_mix_post[layer], seq, tm=512)
        h = _mlp(h, g_mlp_pre[layer], w_ff1, w_ff2, layer, g_mlp_post[layer], tm=1024, tf=512)
    return h.reshape(nb, seq, d)
```
